```python
import jax, jax.numpy as jnp
from jax import lax
import numpy as np

D_MODEL = 1024
BATCH = 2
SEQ = 16384
DEPTH = 2

SC_WIDTH = D_MODEL // 4
SC_GROUPS = 4
SC_KERNEL = 3
SB_HEAD_DIM = 64
SB_HEADS = (D_MODEL // 4) // SB_HEAD_DIM
SB_WIDTH = SB_HEADS * SB_HEAD_DIM
SB_BLOCK = 128
SSM_INNER = D_MODEL // 2
SSM_HEAD_DIM = 64
SSM_HEADS = SSM_INNER // SSM_HEAD_DIM
SSM_GROUPS = 2
SSM_STATE = 64
SSM_CONV = 4
SSM_CHUNK = 256
SSM_CONV_DIM = SSM_INNER + 2 * SSM_GROUPS * SSM_STATE
N_BRANCH = 3
FFN_HIDDEN = -(-8 * D_MODEL // (3 * 256)) * 256
NORM_EPS = 1e-6
N_MOD = 6
PROJ_SIZES = (SC_WIDTH, SC_WIDTH, SC_WIDTH,
              SB_WIDTH, SB_WIDTH, SB_WIDTH,
              SSM_INNER, SSM_CONV_DIM, SSM_HEADS,
              D_MODEL, D_MODEL, D_MODEL)
IN_PROJ = sum(PROJ_SIZES)

kernel_name = "hybrid_shortconv_stickbreak_ssd_block"


def rms_norm(x, g):
    x32 = x.astype(jnp.float32)
    y = x32 * lax.rsqrt(jnp.mean(x32 * x32, axis=-1, keepdims=True) + NORM_EPS)
    return (y * g.astype(jnp.float32)).astype(x.dtype)


def causal_depthwise_conv(x, w):
    k = w.shape[0]
    return lax.conv_general_dilated(
        x, w[:, None, :].astype(x.dtype), window_strides=(1,), padding=[(k - 1, 0)],
        dimension_numbers=("NWC", "WIO", "NWC"), feature_group_count=x.shape[-1])


def split_columns(p):
    offsets = [int(o) for o in np.cumsum(PROJ_SIZES)[:-1]]
    return jnp.split(p, offsets, axis=-1)


def short_conv_mixer(b_gate, c_gate, xa, w_conv):
    return b_gate * causal_depthwise_conv(c_gate * xa, w_conv)


def stick_breaking_attention(q, k, v):
    bsz, seq, heads, dh = q.shape
    n_blk = seq // SB_BLOCK
    scale = dh ** -0.5
    qh = q.transpose(0, 2, 1, 3)
    kh = k.transpose(0, 2, 1, 3)
    vh = v.transpose(0, 2, 1, 3)
    strict = jnp.tril(jnp.ones((SB_BLOCK, SB_BLOCK), jnp.float32), -1)
    outs = []
    for i in range(n_blk):
        start, end = i * SB_BLOCK, (i + 1) * SB_BLOCK
        z = jnp.einsum("bhqd,bhkd->bhqk", qh[:, :, start:end], kh[:, :, :end],
                       preferred_element_type=jnp.float32) * scale
        mask = jnp.arange(end)[None, :] < (start + jnp.arange(SB_BLOCK))[:, None]
        log_keep = jnp.where(mask, jax.nn.log_sigmoid(-z), 0.0)
        lk = log_keep.reshape(bsz, heads, SB_BLOCK, i + 1, SB_BLOCK)
        within = jnp.einsum("bhqcj,js->bhqcs", lk, strict)
        blk_tot = jnp.sum(lk, axis=-1)
        later = lax.cumsum(blk_tot, axis=3, reverse=True) - blk_tot
        log_rest = (within + later[..., None]).reshape(bsz, heads, SB_BLOCK, end)
        att = jnp.exp(jnp.where(mask, z + log_keep + log_rest, -jnp.inf))
        outs.append(jnp.einsum("bhqk,bhkd->bhqd", att.astype(vh.dtype), vh[:, :, :end]))
    out = jnp.concatenate(outs, axis=2)
    return out.transpose(0, 2, 1, 3).reshape(bsz, seq, heads * dh)


def segsum_exp(a_cs):
    l = a_cs.shape[-1]
    mask = jnp.tril(jnp.ones((l, l), dtype=bool))
    diff = a_cs[..., :, None] - a_cs[..., None, :]
    return jnp.exp(jnp.where(mask, diff, -jnp.inf))


def ssd_scan(xh, dt, a_neg, bm, cm):
    bsz, seq, heads, hd = xh.shape
    reps = heads // bm.shape[2]
    bh = jnp.repeat(bm, reps, axis=2)
    ch = jnp.repeat(cm, reps, axis=2)
    xdt = xh * dt[..., None]
    a = dt * a_neg
    pad = (-seq) % SSM_CHUNK
    if pad:
        pw = ((0, 0), (0, pad), (0, 0), (0, 0))
        xdt, bh, ch = jnp.pad(xdt, pw), jnp.pad(bh, pw), jnp.pad(ch, pw)
        a = jnp.pad(a, ((0, 0), (0, pad), (0, 0)))
    n_c = (seq + pad) // SSM_CHUNK
    xc = xdt.reshape(bsz, n_c, SSM_CHUNK, heads, hd)
    bc = bh.reshape(bsz, n_c, SSM_CHUNK, heads, -1)
    cc = ch.reshape(bsz, n_c, SSM_CHUNK, heads, -1)
    a_cs = jnp.cumsum(a.reshape(bsz, n_c, SSM_CHUNK, heads), axis=2)
    decay_in = segsum_exp(a_cs.transpose(0, 1, 3, 2))
    scores = jnp.einsum("bclhn,bcshn->bchls", cc, bc) * decay_in
    y_diag = jnp.einsum("bchls,bcshp->bclhp", scores, xc)
    decay_to_end = jnp.exp(a_cs[:, :, -1:, :] - a_cs)
    chunk_states = jnp.einsum("bclhn,bclhp->bchpn", bc * decay_to_end[..., None], xc)
    chunk_decay = jnp.exp(a_cs[:, :, -1, :])

    def step(state, inp):
        s_c, d_c = inp
        return state * d_c[..., None, None] + s_c, state

    init = jnp.zeros((bsz, heads, hd, bc.shape[-1]), jnp.float32)
    _, prev = lax.scan(step, init, (chunk_states.transpose(1, 0, 2, 3, 4),
                                    chunk_decay.transpose(1, 0, 2)))
    prev = prev.transpose(1, 0, 2, 3, 4)
    y_off = jnp.einsum("bclhn,bchpn->bclhp", cc, prev) * jnp.exp(a_cs)[..., None]
    return (y_diag + y_off).reshape(bsz, seq + pad, heads, hd)[:, :seq]


def mamba2_mixer(z, xbc, dt_raw, conv_w, conv_b, dt_bias, a_log, d_skip, norm_w):
    xbc = jax.nn.silu(causal_depthwise_conv(xbc, conv_w) + conv_b)
    xs, bm, cm = jnp.split(xbc, [SSM_INNER, SSM_INNER + SSM_GROUPS * SSM_STATE], axis=-1)
    bsz, seq, _ = xs.shape
    xh = xs.reshape(bsz, seq, SSM_HEADS, SSM_HEAD_DIM).astype(jnp.float32)
    bm = bm.reshape(bsz, seq, SSM_GROUPS, SSM_STATE).astype(jnp.float32)
    cm = cm.reshape(bsz, seq, SSM_GROUPS, SSM_STATE).astype(jnp.float32)
    dt = jax.nn.softplus(dt_raw.astype(jnp.float32) + dt_bias.astype(jnp.float32))
    a_neg = -jnp.exp(a_log.astype(jnp.float32))
    y = ssd_scan(xh, dt, a_neg, bm, cm) + xh * d_skip.astype(jnp.float32)[:, None]
    y = y.reshape(bsz, seq, SSM_INNER) * jax.nn.silu(z.astype(jnp.float32))
    yg = y.reshape(bsz, seq, SSM_GROUPS, SSM_INNER // SSM_GROUPS)
    yg = yg * lax.rsqrt(jnp.mean(yg * yg, axis=-1, keepdims=True) + NORM_EPS)
    return (yg.reshape(bsz, seq, SSM_INNER) * norm_w.astype(jnp.float32)).astype(z.dtype)


def hybrid_layer(x, c, mod_w, mod_b, g_pre_mix, g_post_mix, g_pre_ffn, g_post_ffn,
                 w_in, sc_conv_w, ssm_conv_w, ssm_conv_b, ssm_dt_bias, ssm_a_log, ssm_d,
                 ssm_norm_w, w_sc_out, w_sb_out, w_ssm_out, w_o, w_ffn_in, w_ffn_out):
    bsz, seq, _ = x.shape
    mod = jax.nn.silu(c) @ mod_w + mod_b
    shift1, scale1, gate1, shift2, scale2, gate2 = [m[:, None, :] for m in jnp.split(mod, N_MOD, axis=-1)]

    h = rms_norm(x, g_pre_mix) * (1 + scale1) + shift1
    (sc_b, sc_c, sc_x, q, k, v, z, xbc, dt_raw,
     gl_a, gl_b, gl_c) = split_columns(h @ w_in)
    y_a = short_conv_mixer(sc_b, sc_c, sc_x, sc_conv_w) @ w_sc_out
    y_b = stick_breaking_attention(
        q.reshape(bsz, seq, SB_HEADS, SB_HEAD_DIM),
        k.reshape(bsz, seq, SB_HEADS, SB_HEAD_DIM),
        v.reshape(bsz, seq, SB_HEADS, SB_HEAD_DIM)) @ w_sb_out
    y_c = mamba2_mixer(z, xbc, dt_raw, ssm_conv_w, ssm_conv_b, ssm_dt_bias, ssm_a_log,
                       ssm_d, ssm_norm_w) @ w_ssm_out
    merged = (jax.nn.sigmoid(gl_a) * y_a + jax.nn.sigmoid(gl_b) * y_b
              + jax.nn.sigmoid(gl_c) * y_c)
    mix_out = merged @ w_o
    x = x + (gate1 * rms_norm(mix_out, g_post_mix)).astype(x.dtype)

    h2 = rms_norm(x, g_pre_ffn) * (1 + scale2) + shift2
    gt, up = jnp.split(h2 @ w_ffn_in, 2, axis=-1)
    f = (jax.nn.silu(gt) * up) @ w_ffn_out
    x = x + (gate2 * rms_norm(f, g_post_ffn)).astype(x.dtype)
    return x


def setup_inputs(seed: int = 0) -> dict:
    key = jax.random.key(seed)
    ks = jax.random.split(key, 24)
    f32 = jnp.float32

    def nrm(k, shape, fan_in):
        return jax.random.normal(k, shape, f32) * (fan_in ** -0.5)

    def gain(k, shape):
        return 1.0 + 0.05 * jax.random.normal(k, shape, f32)

    dt0 = jnp.exp(jax.random.uniform(ks[14], (DEPTH, SSM_HEADS), f32,
                                     jnp.log(1e-3), jnp.log(1e-1)))
    return {
        "x": jax.random.normal(ks[0], (BATCH, SEQ, D_MODEL), f32),
        "c": jax.random.normal(ks[1], (BATCH, D_MODEL), f32),
        "mod_w": nrm(ks[2], (DEPTH, D_MODEL, N_MOD * D_MODEL), D_MODEL),
        "mod_b": 0.02 * jax.random.normal(ks[3], (DEPTH, N_MOD * D_MODEL), f32),
        "g_pre_mix": gain(ks[4], (DEPTH, D_MODEL)),
        "g_post_mix": gain(ks[5], (DEPTH, D_MODEL)),
        "g_pre_ffn": gain(ks[6], (DEPTH, D_MODEL)),
        "g_post_ffn": gain(ks[7], (DEPTH, D_MODEL)),
        "w_in": nrm(ks[8], (DEPTH, D_MODEL, IN_PROJ), D_MODEL),
        "sc_conv_w": nrm(ks[9], (DEPTH, SC_KERNEL, SC_WIDTH), SC_KERNEL),
        "ssm_conv_w": nrm(ks[10], (DEPTH, SSM_CONV, SSM_CONV_DIM), SSM_CONV),
        "ssm_conv_b": 0.02 * jax.random.normal(ks[11], (DEPTH, SSM_CONV_DIM), f32),
        "ssm_dt_bias": dt0 + jnp.log(-jnp.expm1(-dt0)),
        "ssm_a_log": jnp.log(jax.random.uniform(ks[12], (DEPTH, SSM_HEADS), f32, 1.0, 16.0)),
        "ssm_d": 1.0 + 0.1 * jax.random.normal(ks[13], (DEPTH, SSM_HEADS), f32),
        "ssm_norm_w": gain(ks[15], (DEPTH, SSM_INNER)),
        "w_sc_out": nrm(ks[16], (DEPTH, SC_WIDTH, D_MODEL), SC_WIDTH),
        "w_sb_out": nrm(ks[17], (DEPTH, SB_WIDTH, D_MODEL), SB_WIDTH),
        "w_ssm_out": nrm(ks[18], (DEPTH, SSM_INNER, D_MODEL), SSM_INNER),
        "w_o": nrm(ks[19], (DEPTH, D_MODEL, D_MODEL), D_MODEL),
        "w_ffn_in": nrm(ks[20], (DEPTH, D_MODEL, 2 * FFN_HIDDEN), D_MODEL),
        "w_ffn_out": nrm(ks[21], (DEPTH, FFN_HIDDEN, D_MODEL), FFN_HIDDEN),
    }


def reference(x, c, mod_w, mod_b, g_pre_mix, g_post_mix, g_pre_ffn, g_post_ffn, w_in,
              sc_conv_w, ssm_conv_w, ssm_conv_b, ssm_dt_bias, ssm_a_log, ssm_d, ssm_norm_w,
              w_sc_out, w_sb_out, w_ssm_out, w_o, w_ffn_in, w_ffn_out):
    for l in range(DEPTH):
        x = hybrid_layer(x, c, mod_w[l], mod_b[l], g_pre_mix[l], g_post_mix[l],
                         g_pre_ffn[l], g_post_ffn[l], w_in[l], sc_conv_w[l], ssm_conv_w[l],
                         ssm_conv_b[l], ssm_dt_bias[l], ssm_a_log[l], ssm_d[l], ssm_norm_w[l],
                         w_sc_out[l], w_sb_out[l], w_ssm_out[l], w_o[l], w_ffn_in[l],
                         w_ffn_out[l])
    return x
```

```python
import functools

import jax
import jax.numpy as jnp
from jax import lax
from jax.experimental import pallas as pl
from jax.experimental.pallas import tpu as pltpu

D_MODEL = 1024
SC_WIDTH = 256
SC_KERNEL = 3
SB_HEAD_DIM = 64
SB_WIDTH = 256
SB_BLOCK = 128
SSM_INNER = 512
SSM_HEAD_DIM = 64
SSM_HEADS = 8
SSM_GROUPS = 2
SSM_STATE = 64
SSM_CONV = 4
SSM_CHUNK = 256
SSM_CONV_DIM = SSM_INNER + 2 * SSM_GROUPS * SSM_STATE
FFN_HIDDEN = 2816
NORM_EPS = 1e-6
N_MOD = 6

LANES = 128
SUBLANES = 8
VMEM_LIMIT_BYTES = 56 * 1024 * 1024

TOKEN_TILE = 512
ATTN_Q_BLOCKS = 4
HALO = SUBLANES

F32 = jnp.float32
BF16 = jnp.bfloat16


def _compiler_params(n_axes):
    return pltpu.CompilerParams(
        dimension_semantics=("arbitrary",) * n_axes,
        vmem_limit_bytes=VMEM_LIMIT_BYTES,
    )


def _sigmoid(v):
    return 1.0 / (1.0 + jnp.exp(-v))


def _softplus(v):
    return jnp.maximum(v, 0.0) + jnp.log(1.0 + jnp.exp(-jnp.abs(v)))


def _modulated_prenorm(x, g, scale, shift):
    ms = jnp.mean(x * x, axis=-1, keepdims=True)
    return x * lax.rsqrt(ms + NORM_EPS) * (g * (1.0 + scale)) + shift


def _mod_kernel(c_ref, w_ref, b_ref, o_ref):
    c = c_ref[...]
    sc = c * _sigmoid(c)
    o_ref[0] = jnp.dot(sc, w_ref[0], preferred_element_type=F32,
                       precision=lax.Precision.HIGHEST) + b_ref[0]


def _modulation(c, mod_w, mod_b):
    depth, d, n_out = mod_w.shape
    bsz = c.shape[0]
    tn = D_MODEL
    return pl.pallas_call(
        _mod_kernel,
        grid=(depth, n_out // tn),
        in_specs=[
            pl.BlockSpec((bsz, d), lambda l, j: (0, 0)),
            pl.BlockSpec((1, d, tn), lambda l, j: (l, 0, j)),
            pl.BlockSpec((1, 1, tn), lambda l, j: (l, 0, j)),
        ],
        out_specs=pl.BlockSpec((1, bsz, tn), lambda l, j: (l, 0, j)),
        out_shape=jax.ShapeDtypeStruct((depth, bsz, n_out), F32),
        compiler_params=_compiler_params(2),
        name="adaln_modulation",
    )(c, mod_w, mod_b.reshape(depth, 1, n_out))


_A_COLS = 3 * SC_WIDTH
_QKV_COLS = 3 * SB_WIDTH
_OFF_QKV = _A_COLS
_OFF_Z = _OFF_QKV + _QKV_COLS
_OFF_XBC = _OFF_Z + SSM_INNER
_OFF_DT = _OFF_XBC + SSM_CONV_DIM
_W1_COLS = _OFF_DT + LANES


def _inproj_kernel(tiles_per_seq, x_ref, mod_ref, g_ref, w_ref, cw_ref,
                   fa_ref, qkv_ref, z_ref, xbc_ref, dt_ref, ubuf):
    i = pl.program_id(0)
    tm = x_ref.shape[0]
    shift = mod_ref[0, 0:1, :]
    scale = mod_ref[0, 1:2, :]
    h = _modulated_prenorm(x_ref[...], g_ref[...], scale, shift).astype(BF16)

    def proj(lo, hi):
        return jnp.dot(h, w_ref[:, lo:hi], preferred_element_type=F32)

    qkv_ref[...] = proj(_OFF_QKV, _OFF_Z).astype(BF16)
    z_ref[...] = proj(_OFF_Z, _OFF_XBC).astype(BF16)
    xbc_ref[...] = proj(_OFF_XBC, _OFF_DT).astype(BF16)
    dt_ref[...] = proj(_OFF_DT, _W1_COLS)

    pa = proj(0, _A_COLS)
    u = pa[:, SC_WIDTH:2 * SC_WIDTH] * pa[:, 2 * SC_WIDTH:]

    @pl.when(i % tiles_per_seq == 0)
    def _():
        ubuf[0:HALO, :] = jnp.zeros((HALO, SC_WIDTH), F32)

    @pl.when(i % tiles_per_seq != 0)
    def _():
        ubuf[0:HALO, :] = ubuf[tm:tm + HALO, :]

    ubuf[HALO:, :] = u
    conv = cw_ref[SC_KERNEL - 1:SC_KERNEL, :] * u
    for k in range(SC_KERNEL - 1):
        back = SC_KERNEL - 1 - k
        conv = conv + cw_ref[k:k + 1, :] * ubuf[HALO - back:HALO - back + tm, :]
    fa_ref[...] = (pa[:, :SC_WIDTH] * conv).astype(BF16)


def _inproj(xf, mod_l, g_pre, w1, sc_conv_w, seq_len):
    n, d = xf.shape
    tm = min(TOKEN_TILE, seq_len)
    tiles_per_seq = seq_len // tm
    row = lambda i: (i, 0)
    const = lambda i: (0, 0)
    return pl.pallas_call(
        functools.partial(_inproj_kernel, tiles_per_seq),
        grid=(n // tm,),
        in_specs=[
            pl.BlockSpec((tm, d), row),
            pl.BlockSpec((1, N_MOD, d), lambda i: (i // tiles_per_seq, 0, 0)),
            pl.BlockSpec((1, d), const),
            pl.BlockSpec((d, _W1_COLS), const),
            pl.BlockSpec((SC_KERNEL, SC_WIDTH), const),
        ],
        out_specs=[
            pl.BlockSpec((tm, SC_WIDTH), row),
            pl.BlockSpec((tm, _QKV_COLS), row),
            pl.BlockSpec((tm, SSM_INNER), row),
            pl.BlockSpec((tm, SSM_CONV_DIM), row),
            pl.BlockSpec((tm, LANES), row),
        ],
        out_shape=[
            jax.ShapeDtypeStruct((n, SC_WIDTH), BF16),
            jax.ShapeDtypeStruct((n, _QKV_COLS), BF16),
            jax.ShapeDtypeStruct((n, SSM_INNER), BF16),
            jax.ShapeDtypeStruct((n, SSM_CONV_DIM), BF16),
            jax.ShapeDtypeStruct((n, LANES), F32),
        ],
        scratch_shapes=[pltpu.VMEM((tm + HALO, SC_WIDTH), F32)],
        compiler_params=_compiler_params(1),
        name="inproj_shortconv",
    )(xf, mod_l, g_pre, w1, sc_conv_w)


def _attn_kernel(q_ref, k_ref, v_ref, o_ref, acc, carry):
    blk = SB_BLOCK
    qb0 = pl.program_id(2) * ATTN_Q_BLOCKS
    lane = lax.broadcasted_iota(jnp.int32, (blk, LANES), 1)
    low_half = lane < SB_HEAD_DIM
    rows = lax.broadcasted_iota(jnp.int32, (blk, blk), 0)
    cols = lax.broadcasted_iota(jnp.int32, (blk, blk), 1)
    causal = cols < rows
    jj = lax.broadcasted_iota(jnp.int32, (blk, 2 * blk), 0)
    ss = lax.broadcasted_iota(jnp.int32, (blk, 2 * blk), 1)
    cum_mat = jnp.where((jj > ss) | (ss >= blk), 1.0, 0.0).astype(BF16)

    def tile(qh, j, head, diagonal):
        start = pl.multiple_of(j * blk, blk)
        kj = k_ref[pl.ds(start, blk), :]
        vj = v_ref[pl.ds(start, blk), :]
        z = lax.dot_general(qh, kj, (((1,), (1,)), ((), ())), preferred_element_type=F32)
        lk = -_softplus(z)
        if diagonal:
            lk = jnp.where(causal, lk, 0.0)
        wt = jnp.dot(lk.astype(BF16), cum_mat, preferred_element_type=F32)
        arg = z + lk + wt[:, :blk]
        if diagonal:
            att = jnp.where(causal, jnp.exp(arg), 0.0)
            carry[head] = wt[:, blk:]
            acc[head] = jnp.dot(att.astype(BF16), vj, preferred_element_type=F32)
        else:
            att = jnp.exp(arg + carry[head])
            carry[head] = carry[head] + wt[:, blk:]
            acc[head] = acc[head] + jnp.dot(att.astype(BF16), vj, preferred_element_type=F32)

    for qb in range(ATTN_Q_BLOCKS):
        i_blk = qb0 + qb
        q = q_ref[qb * blk:(qb + 1) * blk, :]
        q_heads = (jnp.where(low_half, q, jnp.zeros_like(q)), jnp.where(low_half, jnp.zeros_like(q), q))
        for head in range(2):
            tile(q_heads[head], i_blk, head, True)

        def body(t, _):
            j = i_blk - 1 - t
            for head in range(2):
                tile(q_heads[head], j, head, False)
            return 0

        lax.fori_loop(0, i_blk, body, 0)
        o_ref[qb * blk:(qb + 1) * blk, :] = jnp.where(low_half, acc[0], acc[1]).astype(BF16)


def _attention(qkv, bsz, seq_len):
    n = qkv.shape[0]
    rows = ATTN_Q_BLOCKS * SB_BLOCK
    steps = seq_len // rows
    pairs = SB_WIDTH // LANES
    return pl.pallas_call(
        _attn_kernel,
        grid=(bsz, pairs, steps),
        in_specs=[
            pl.BlockSpec((rows, LANES), lambda b, p, i: (b * steps + i, p)),
            pl.BlockSpec((seq_len, LANES), lambda b, p, i: (b, pairs + p)),
            pl.BlockSpec((seq_len, LANES), lambda b, p, i: (b, 2 * pairs + p)),
        ],
        out_specs=pl.BlockSpec((rows, LANES), lambda b, p, i: (b * steps + i, p)),
        out_shape=jax.ShapeDtypeStruct((n, SB_WIDTH), BF16),
        scratch_shapes=[pltpu.VMEM((2, SB_BLOCK, LANES), F32), pltpu.VMEM((2, SB_BLOCK, SB_BLOCK), F32)],
        compiler_params=_compiler_params(3),
        name="stickbreak_attention",
    )(qkv, qkv, qkv)


def _ssd_kernel(z_ref, xbc_ref, dt_ref, cw_ref, cb_ref, dtb_ref, alog_ref, dskip_ref, nw_ref,
                y_ref, xbuf, state):
    lc = SSM_CHUNK
    c = pl.program_id(1)

    @pl.when(c == 0)
    def _():
        xbuf[0:HALO, :] = jnp.zeros((HALO, SSM_CONV_DIM), F32)
        state[...] = jnp.zeros_like(state)

    @pl.when(c != 0)
    def _():
        xbuf[0:HALO, :] = xbuf[lc:lc + HALO, :]

    xbuf[HALO:, :] = xbc_ref[...].astype(F32)
    conv = cb_ref[...] + cw_ref[SSM_CONV - 1:SSM_CONV, :] * xbuf[HALO:, :]
    for k in range(SSM_CONV - 1):
        back = SSM_CONV - 1 - k
        conv = conv + cw_ref[k:k + 1, :] * xbuf[HALO - back:HALO - back + lc, :]
    xc = conv * _sigmoid(conv)
    xs = xc[:, :SSM_INNER]
    bm = xc[:, SSM_INNER:SSM_INNER + LANES]
    cm = xc[:, SSM_INNER + LANES:]

    dt = _softplus(dt_ref[...] + dtb_ref[...])
    a = dt * -jnp.exp(alog_ref[...])
    ri = lax.broadcasted_iota(jnp.int32, (lc, lc), 0)
    ci = lax.broadcasted_iota(jnp.int32, (lc, lc), 1)
    lower = ri >= ci
    a_cs = jnp.dot(jnp.where(lower, 1.0, 0.0), a, preferred_element_type=F32,
                   precision=lax.Precision.HIGHEST)
    a_cs_t = a_cs.T

    lane = lax.broadcasted_iota(jnp.int32, (lc, LANES), 1)
    low_half = lane < SSM_HEAD_DIM

    def expand(col_src):
        parts = []
        for p in range(SSM_HEADS // 2):
            lo = jnp.broadcast_to(col_src[:, 2 * p:2 * p + 1], (lc, LANES))
            hi = jnp.broadcast_to(col_src[:, 2 * p + 1:2 * p + 2], (lc, LANES))
            parts.append(jnp.where(low_half, lo, hi))
        return jnp.concatenate(parts, axis=1)

    dt_x = expand(dt)
    acs_x = expand(a_cs)
    xdt = xs * dt_x
    last_x = acs_x[lc - 1:lc, :]
    decay_to_end = jnp.exp(last_x - acs_x)
    decay_from_start = jnp.exp(acs_x)
    chunk_decay = jnp.exp(last_x)

    bm16 = bm.astype(BF16)
    cm16 = cm.astype(BF16)
    cb = []
    for g in range(SSM_GROUPS):
        cg = jnp.where(low_half if g == 0 else ~low_half, cm16, jnp.zeros_like(cm16))
        cb.append(lax.dot_general(cg, bm16, (((1,), (1,)), ((), ())), preferred_element_type=F32))
    xdt16 = xdt.astype(BF16)
    y_pairs = []
    for p in range(SSM_HEADS // 2):
        xp = xdt16[:, p * LANES:(p + 1) * LANES]
        yp = None
        for half in range(2):
            hd = 2 * p + half
            g = hd // (SSM_HEADS // SSM_GROUPS)
            diff = a_cs[:, hd:hd + 1] - a_cs_t[hd:hd + 1, :]
            dec = jnp.exp(jnp.where(lower, diff, -jnp.inf))
            scores = (cb[g] * dec).astype(BF16)
            xh = jnp.where(low_half if half == 0 else ~low_half, xp, jnp.zeros_like(xp))
            t = jnp.dot(scores, xh, preferred_element_type=F32)
            yp = t if yp is None else yp + t
        y_pairs.append(yp)
    y = jnp.concatenate(y_pairs, axis=1)

    prev = state[...]
    y = y + jnp.dot(cm16, prev.astype(BF16), preferred_element_type=F32) * decay_from_start
    xw = (xdt * decay_to_end).astype(BF16)
    cs = lax.dot_general(bm16, xw, (((0,), (0,)), ((), ())), preferred_element_type=F32)
    srow = lax.broadcasted_iota(jnp.int32, cs.shape, 0) // SSM_STATE
    scol = lax.broadcasted_iota(jnp.int32, cs.shape, 1) // (SSM_INNER // SSM_GROUPS)
    state[...] = prev * chunk_decay + jnp.where(srow == scol, cs, 0.0)

    y = y + xs * dskip_ref[...]
    zf = z_ref[...].astype(F32)
    y = y * (zf * _sigmoid(zf))
    gw = SSM_INNER // SSM_GROUPS
    outs = []
    for g in range(SSM_GROUPS):
        yg = y[:, g * gw:(g + 1) * gw]
        ms = jnp.mean(yg * yg, axis=-1, keepdims=True)
        outs.append(yg * lax.rsqrt(ms + NORM_EPS))
    y_ref[...] = (jnp.concatenate(outs, axis=1) * nw_ref[...]).astype(BF16)


def _ssd(z, xbc, dt, conv_w, conv_b, dt_bias, a_log, d_skip, norm_w, bsz, seq_len):
    n = z.shape[0]
    lc = SSM_CHUNK
    chunks = seq_len // lc
    row = lambda b, c: (b * chunks + c, 0)
    const = lambda b, c: (0, 0)
    return pl.pallas_call(
        _ssd_kernel,
        grid=(bsz, chunks),
        in_specs=[
            pl.BlockSpec((lc, SSM_INNER), row),
            pl.BlockSpec((lc, SSM_CONV_DIM), row),
            pl.BlockSpec((lc, LANES), row),
            pl.BlockSpec((SSM_CONV, SSM_CONV_DIM), const),
            pl.BlockSpec((1, SSM_CONV_DIM), const),
            pl.BlockSpec((1, LANES), const),
            pl.BlockSpec((1, LANES), const),
            pl.BlockSpec((1, SSM_INNER), const),
            pl.BlockSpec((1, SSM_INNER), const),
        ],
        out_specs=pl.BlockSpec((lc, SSM_INNER), row),
        out_shape=jax.ShapeDtypeStruct((n, SSM_INNER), BF16),
        scratch_shapes=[pltpu.VMEM((lc + HALO, SSM_CONV_DIM), F32),
                        pltpu.VMEM((SSM_GROUPS * SSM_STATE, SSM_INNER), F32)],
        compiler_params=_compiler_params(2),
        name="ssd_mixer",
    )(z, xbc, dt, conv_w, conv_b, dt_bias, a_log, d_skip, norm_w)


def _merge_kernel(x_ref, mod_ref, gpre_ref, gpost_ref, fa_ref, fb_ref, fc_ref,
                  wg_ref, wa_ref, wb_ref, wc_ref, wo_ref, o_ref):
    d = D_MODEL
    x = x_ref[...]
    shift = mod_ref[0, 0:1, :]
    scale = mod_ref[0, 1:2, :]
    gate = mod_ref[0, 2:3, :]
    h = _modulated_prenorm(x, gpre_ref[...], scale, shift).astype(BF16)
    merged = None
    for br, (f_ref, w_ref) in enumerate(((fa_ref, wa_ref), (fb_ref, wb_ref), (fc_ref, wc_ref))):
        gl = jnp.dot(h, wg_ref[:, br * d:(br + 1) * d], preferred_element_type=F32)
        yb = jnp.dot(f_ref[...], w_ref[...], preferred_element_type=F32)
        t = _sigmoid(gl) * yb
        merged = t if merged is None else merged + t
    mix = jnp.dot(merged.astype(BF16), wo_ref[...], preferred_element_type=F32)
    ms = jnp.mean(mix * mix, axis=-1, keepdims=True)
    o_ref[...] = x + gate * (mix * lax.rsqrt(ms + NORM_EPS) * gpost_ref[...])


def _merge(xf, mod_l, g_pre, g_post, fa, fb, fc, wg, wa, wb, wc, wo, seq_len):
    n, d = xf.shape
    tm = min(TOKEN_TILE, seq_len)
    tiles_per_seq = seq_len // tm
    row = lambda i: (i, 0)
    const = lambda i: (0, 0)
    return pl.pallas_call(
        _merge_kernel,
        grid=(n // tm,),
        in_specs=[
            pl.BlockSpec((tm, d), row),
            pl.BlockSpec((1, N_MOD, d), lambda i: (i // tiles_per_seq, 0, 0)),
            pl.BlockSpec((1, d), const),
            pl.BlockSpec((1, d), const),
            pl.BlockSpec((tm, SC_WIDTH), row),
            pl.BlockSpec((tm, SB_WIDTH), row),
            pl.BlockSpec((tm, SSM_INNER), row),
            pl.BlockSpec((d, 3 * d), const),
            pl.BlockSpec((SC_WIDTH, d), const),
            pl.BlockSpec((SB_WIDTH, d), const),
            pl.BlockSpec((SSM_INNER, d), const),
            pl.BlockSpec((d, d), const),
        ],
        out_specs=pl.BlockSpec((tm, d), row),
        out_shape=jax.ShapeDtypeStruct((n, d), F32),
        compiler_params=_compiler_params(1),
        name="merge_outproj",
    )(xf, mod_l, g_pre, g_post, fa, fb, fc, wg, wa, wb, wc, wo)


def _ffn_kernel(x_ref, mod_ref, gpre_ref, gpost_ref, win_ref, wout_ref, o_ref):
    x = x_ref[...]
    shift = mod_ref[0, 3:4, :]
    scale = mod_ref[0, 4:5, :]
    gate = mod_ref[0, 5:6, :]
    h = _modulated_prenorm(x, gpre_ref[...], scale, shift).astype(BF16)
    gt = jnp.dot(h, win_ref[:, :FFN_HIDDEN], preferred_element_type=F32)
    up = jnp.dot(h, win_ref[:, FFN_HIDDEN:], preferred_element_type=F32)
    act = (gt * _sigmoid(gt) * up).astype(BF16)
    f = jnp.dot(act, wout_ref[...], preferred_element_type=F32)
    ms = jnp.mean(f * f, axis=-1, keepdims=True)
    o_ref[...] = x + gate * (f * lax.rsqrt(ms + NORM_EPS) * gpost_ref[...])


def _ffn(xf, mod_l, g_pre, g_post, w_in, w_out, seq_len):
    n, d = xf.shape
    tm = min(TOKEN_TILE, seq_len)
    tiles_per_seq = seq_len // tm
    row = lambda i: (i, 0)
    const = lambda i: (0, 0)
    return pl.pallas_call(
        _ffn_kernel,
        grid=(n // tm,),
        in_specs=[
            pl.BlockSpec((tm, d), row),
            pl.BlockSpec((1, N_MOD, d), lambda i: (i // tiles_per_seq, 0, 0)),
            pl.BlockSpec((1, d), const),
            pl.BlockSpec((1, d), const),
            pl.BlockSpec((d, 2 * FFN_HIDDEN), const),
            pl.BlockSpec((FFN_HIDDEN, d), const),
        ],
        out_specs=pl.BlockSpec((tm, d), row),
        out_shape=jax.ShapeDtypeStruct((n, d), F32),
        compiler_params=_compiler_params(1),
        name="swiglu_ffn",
    )(xf, mod_l, g_pre, g_post, w_in, w_out)


def _pad_lanes(v):
    return jnp.pad(v.astype(F32), (0, LANES - v.shape[0])).reshape(1, LANES)


def kernel(x, c, mod_w, mod_b, g_pre_mix, g_post_mix, g_pre_ffn, g_post_ffn, w_in, sc_conv_w,
           ssm_conv_w, ssm_conv_b, ssm_dt_bias, ssm_a_log, ssm_d, ssm_norm_w, w_sc_out, w_sb_out,
           w_ssm_out, w_o, w_ffn_in, w_ffn_out):
    bsz, seq_len, d = x.shape
    depth = mod_w.shape[0]
    assert d == D_MODEL and seq_len % (ATTN_Q_BLOCKS * SB_BLOCK) == 0 and seq_len % SSM_CHUNK == 0
    mod = _modulation(c, mod_w, mod_b).reshape(depth, bsz, N_MOD, d)
    xf = x.reshape(bsz * seq_len, d)
    q_scale = SB_HEAD_DIM ** -0.5
    for l in range(depth):
        wl = w_in[l]
        w1 = jnp.concatenate([
            wl[:, :_OFF_QKV],
            wl[:, _OFF_QKV:_OFF_QKV + SB_WIDTH] * q_scale,
            wl[:, _OFF_QKV + SB_WIDTH:_OFF_DT],
            jnp.pad(wl[:, _OFF_DT:_OFF_DT + SSM_HEADS], ((0, 0), (0, LANES - SSM_HEADS))),
        ], axis=1).astype(BF16)
        wg = wl[:, _OFF_DT + SSM_HEADS:].astype(BF16)
        row = lambda v: v.reshape(1, -1).astype(F32)

        fa, qkv, z, xbc, dt = _inproj(xf, mod[l], row(g_pre_mix[l]), w1, sc_conv_w[l].astype(F32), seq_len)
        fb = _attention(qkv, bsz, seq_len)
        fc = _ssd(z, xbc, dt, ssm_conv_w[l].astype(F32), row(ssm_conv_b[l]), _pad_lanes(ssm_dt_bias[l]),
                  _pad_lanes(ssm_a_log[l]),
                  row(jnp.repeat(ssm_d[l], SSM_HEAD_DIM)), row(ssm_norm_w[l]), bsz, seq_len)
        xf = _merge(xf, mod[l], row(g_pre_mix[l]), row(g_post_mix[l]), fa, fb, fc, wg,
                    w_sc_out[l].astype(BF16), w_sb_out[l].astype(BF16), w_ssm_out[l].astype(BF16),
                    w_o[l].astype(BF16), seq_len)
        xf = _ffn(xf, mod[l], row(g_pre_ffn[l]), row(g_post_ffn[l]), w_ffn_in[l].astype(BF16),
                  w_ffn_out[l].astype(BF16), seq_len)
    return xf.reshape(bsz, seq_len, d)
```

```python
import functools

import jax
import jax.numpy as jnp
from jax import lax
from jax.experimental import pallas as pl
from jax.experimental.pallas import tpu as pltpu

D_MODEL = 1024
SC_WIDTH = 256
SC_KERNEL = 3
SB_HEAD_DIM = 64
SB_WIDTH = 256
SB_BLOCK = 128
SSM_INNER = 512
SSM_HEAD_DIM = 64
SSM_HEADS = 8
SSM_GROUPS = 2
SSM_STATE = 64
SSM_CONV = 4
SSM_CHUNK = 256
SSM_CONV_DIM = SSM_INNER + 2 * SSM_GROUPS * SSM_STATE
FFN_HIDDEN = 2816
NORM_EPS = 1e-6
LOG2_E = 1.4426950408889634
N_MOD = 6

LANES = 128
SUBLANES = 8
VMEM_LIMIT_BYTES = 56 * 1024 * 1024

TOKEN_TILE = 512
ATTN_Q_BLOCKS = 8
ATTN_UNROLL = 4
HALO = SUBLANES

F32 = jnp.float32
BF16 = jnp.bfloat16


def _compiler_params(n_axes):
    return pltpu.CompilerParams(
        dimension_semantics=("arbitrary",) * n_axes,
        vmem_limit_bytes=VMEM_LIMIT_BYTES,
    )


def _sigmoid(v):
    return 1.0 / (1.0 + jnp.exp(-v))


def _softplus(v):
    return jnp.maximum(v, 0.0) + jnp.log(1.0 + jnp.exp(-jnp.abs(v)))


def _modulated_prenorm(x, g, scale, shift):
    ms = jnp.mean(x * x, axis=-1, keepdims=True)
    return x * lax.rsqrt(ms + NORM_EPS) * (g * (1.0 + scale)) + shift


def _mod_kernel(c_ref, w_ref, b_ref, o_ref):
    c = c_ref[...]
    sc = c * _sigmoid(c)
    o_ref[0] = jnp.dot(sc, w_ref[0], preferred_element_type=F32,
                       precision=lax.Precision.HIGHEST) + b_ref[0]


def _modulation(c, mod_w, mod_b):
    depth, d, n_out = mod_w.shape
    bsz = c.shape[0]
    tn = D_MODEL
    return pl.pallas_call(
        _mod_kernel,
        grid=(depth, n_out // tn),
        in_specs=[
            pl.BlockSpec((bsz, d), lambda l, j: (0, 0)),
            pl.BlockSpec((1, d, tn), lambda l, j: (l, 0, j)),
            pl.BlockSpec((1, 1, tn), lambda l, j: (l, 0, j)),
        ],
        out_specs=pl.BlockSpec((1, bsz, tn), lambda l, j: (l, 0, j)),
        out_shape=jax.ShapeDtypeStruct((depth, bsz, n_out), F32),
        compiler_params=_compiler_params(2),
        name="adaln_modulation",
    )(c, mod_w, mod_b.reshape(depth, 1, n_out))


_A_COLS = 3 * SC_WIDTH
_QKV_COLS = 3 * SB_WIDTH
_OFF_QKV = _A_COLS
_OFF_Z = _OFF_QKV + _QKV_COLS
_OFF_XBC = _OFF_Z + SSM_INNER
_OFF_DT = _OFF_XBC + SSM_CONV_DIM
_W1_COLS = _OFF_DT + LANES


def _inproj_kernel(tiles_per_seq, x_ref, mod_ref, g_ref, w_ref, cw_ref,
                   fa_ref, qkv_ref, z_ref, xbc_ref, dt_ref, ubuf):
    i = pl.program_id(0)
    tm = x_ref.shape[0]
    shift = mod_ref[0, 0:1, :]
    scale = mod_ref[0, 1:2, :]
    h = _modulated_prenorm(x_ref[...], g_ref[...], scale, shift).astype(BF16)

    def proj(lo, hi):
        return jnp.dot(h, w_ref[:, lo:hi], preferred_element_type=F32)

    qkv_ref[...] = proj(_OFF_QKV, _OFF_Z).astype(BF16)
    z_ref[...] = proj(_OFF_Z, _OFF_XBC).astype(BF16)
    xbc_ref[...] = proj(_OFF_XBC, _OFF_DT).astype(BF16)
    dt_ref[...] = proj(_OFF_DT, _W1_COLS)

    pa = proj(0, _A_COLS)
    u = pa[:, SC_WIDTH:2 * SC_WIDTH] * pa[:, 2 * SC_WIDTH:]

    @pl.when(i % tiles_per_seq == 0)
    def _():
        ubuf[0:HALO, :] = jnp.zeros((HALO, SC_WIDTH), F32)

    @pl.when(i % tiles_per_seq != 0)
    def _():
        ubuf[0:HALO, :] = ubuf[tm:tm + HALO, :]

    ubuf[HALO:, :] = u
    conv = cw_ref[SC_KERNEL - 1:SC_KERNEL, :] * u
    for k in range(SC_KERNEL - 1):
        back = SC_KERNEL - 1 - k
        conv = conv + cw_ref[k:k + 1, :] * ubuf[HALO - back:HALO - back + tm, :]
    fa_ref[...] = (pa[:, :SC_WIDTH] * conv).astype(BF16)


def _inproj(xf, mod_l, g_pre, w1, sc_conv_w, seq_len):
    n, d = xf.shape
    tm = min(TOKEN_TILE, seq_len)
    tiles_per_seq = seq_len // tm
    row = lambda i: (i, 0)
    const = lambda i: (0, 0)
    return pl.pallas_call(
        functools.partial(_inproj_kernel, tiles_per_seq),
        grid=(n // tm,),
        in_specs=[
            pl.BlockSpec((tm, d), row),
            pl.BlockSpec((1, N_MOD, d), lambda i: (i // tiles_per_seq, 0, 0)),
            pl.BlockSpec((1, d), const),
            pl.BlockSpec((d, _W1_COLS), const),
            pl.BlockSpec((SC_KERNEL, SC_WIDTH), const),
        ],
        out_specs=[
            pl.BlockSpec((tm, SC_WIDTH), row),
            pl.BlockSpec((tm, _QKV_COLS), row),
            pl.BlockSpec((tm, SSM_INNER), row),
            pl.BlockSpec((tm, SSM_CONV_DIM), row),
            pl.BlockSpec((tm, LANES), row),
        ],
        out_shape=[
            jax.ShapeDtypeStruct((n, SC_WIDTH), BF16),
            jax.ShapeDtypeStruct((n, _QKV_COLS), BF16),
            jax.ShapeDtypeStruct((n, SSM_INNER), BF16),
            jax.ShapeDtypeStruct((n, SSM_CONV_DIM), BF16),
            jax.ShapeDtypeStruct((n, LANES), F32),
        ],
        scratch_shapes=[pltpu.VMEM((tm + HALO, SC_WIDTH), F32)],
        compiler_params=_compiler_params(1),
        name="inproj_shortconv",
    )(xf, mod_l, g_pre, w1, sc_conv_w)


def _attn_kernel(q_ref, k_ref, v_ref, o_ref, acc, carry):
    blk = SB_BLOCK
    rows = q_ref.shape[0]
    first_blk = pl.program_id(2) * ATTN_Q_BLOCKS
    q = q_ref[...]

    lane_k = lax.broadcasted_iota(jnp.int32, (blk, LANES), 1)
    low_half = lane_k < SB_HEAD_DIM
    jj = lax.broadcasted_iota(jnp.int32, (2 * blk, 2 * blk), 0)
    ss = lax.broadcasted_iota(jnp.int32, (2 * blk, 2 * blk), 1)
    same_head = (jj // blk) == (ss // blk)
    later_mat = jnp.where(same_head & (jj > ss), 1.0, 0.0).astype(BF16)
    total_mat = jnp.where(same_head, 1.0, 0.0).astype(BF16)
    row_id = lax.broadcasted_iota(jnp.int32, (rows, 2 * blk), 0)
    key_id = lax.broadcasted_iota(jnp.int32, (rows, 2 * blk), 1) % blk

    def stacked(ref, j):
        t = ref[pl.ds(pl.multiple_of(j * blk, blk), blk), :]
        zero = jnp.zeros_like(t)
        return jnp.concatenate([jnp.where(low_half, t, zero), jnp.where(low_half, zero, t)], axis=0)

    def step(j, causal):
        z = lax.dot_general(q, stacked(k_ref, j), (((1,), (1,)), ((), ())), preferred_element_type=F32)
        neg_abs = pltpu.bitcast(pltpu.bitcast(z, jnp.uint32) | jnp.uint32(0x80000000), F32)
        sp = jnp.maximum(z, 0.0) + jnp.log2(1.0 + jnp.exp2(neg_abs))
        log_beta = z - sp
        if causal is not None:
            sp = jnp.where(causal, sp, 0.0)
        sp16 = sp.astype(BF16)
        wp = jnp.dot(sp16, later_mat, preferred_element_type=F32)
        tp = jnp.dot(sp16, total_mat, preferred_element_type=F32)
        c = carry[...]
        att = jnp.exp2(log_beta - (wp + c))
        if causal is not None:
            att = jnp.where(causal, att, 0.0)
        carry[...] = c + tp
        acc[...] += jnp.dot(att.astype(BF16), stacked(v_ref, j), preferred_element_type=F32)

    carry[...] = jnp.zeros_like(carry)
    acc[...] = jnp.zeros_like(acc)
    for kb in range(ATTN_Q_BLOCKS - 1, -1, -1):
        step(first_blk + kb, key_id + kb * blk < row_id)

    def body(t, _):
        for u in range(ATTN_UNROLL):
            step(first_blk - 1 - (t * ATTN_UNROLL + u), None)
        return 0

    lax.fori_loop(0, first_blk // ATTN_UNROLL, body, 0)
    o_ref[...] = acc[...].astype(BF16)


def _attention(qkv, bsz, seq_len):
    n = qkv.shape[0]
    rows = ATTN_Q_BLOCKS * SB_BLOCK
    steps = seq_len // rows
    pairs = SB_WIDTH // LANES
    return pl.pallas_call(
        _attn_kernel,
        grid=(bsz, pairs, steps),
        in_specs=[
            pl.BlockSpec((rows, LANES), lambda b, p, i: (b * steps + i, p)),
            pl.BlockSpec((seq_len, LANES), lambda b, p, i: (b, pairs + p)),
            pl.BlockSpec((seq_len, LANES), lambda b, p, i: (b, 2 * pairs + p)),
        ],
        out_specs=pl.BlockSpec((rows, LANES), lambda b, p, i: (b * steps + i, p)),
        out_shape=jax.ShapeDtypeStruct((n, SB_WIDTH), BF16),
        scratch_shapes=[pltpu.VMEM((rows, LANES), F32), pltpu.VMEM((rows, 2 * SB_BLOCK), F32)],
        compiler_params=_compiler_params(3),
        name="stickbreak_attention",
    )(qkv, qkv, qkv)


def _ssd_kernel(z_ref, xbc_ref, dt_ref, cw_ref, cb_ref, dtb_ref, alog_ref, dskip_ref, nw_ref,
                y_ref, xbuf, state):
    lc = SSM_CHUNK
    c = pl.program_id(1)

    @pl.when(c == 0)
    def _():
        xbuf[0:HALO, :] = jnp.zeros((HALO, SSM_CONV_DIM), F32)
        state[...] = jnp.zeros_like(state)

    @pl.when(c != 0)
    def _():
        xbuf[0:HALO, :] = xbuf[lc:lc + HALO, :]

    xbuf[HALO:, :] = xbc_ref[...].astype(F32)
    conv = cb_ref[...] + cw_ref[SSM_CONV - 1:SSM_CONV, :] * xbuf[HALO:, :]
    for k in range(SSM_CONV - 1):
        back = SSM_CONV - 1 - k
        conv = conv + cw_ref[k:k + 1, :] * xbuf[HALO - back:HALO - back + lc, :]
    xc = conv * _sigmoid(conv)
    xs = xc[:, :SSM_INNER]
    bm = xc[:, SSM_INNER:SSM_INNER + LANES]
    cm = xc[:, SSM_INNER + LANES:]

    dt = _softplus(dt_ref[...] + dtb_ref[...])
    a = dt * -jnp.exp(alog_ref[...])
    ri = lax.broadcasted_iota(jnp.int32, (lc, lc), 0)
    ci = lax.broadcasted_iota(jnp.int32, (lc, lc), 1)
    lower = ri >= ci
    a_cs = jnp.dot(jnp.where(lower, 1.0, 0.0), a, preferred_element_type=F32,
                   precision=lax.Precision.HIGHEST)
    a_cs_t = a_cs.T

    lane = lax.broadcasted_iota(jnp.int32, (lc, LANES), 1)
    low_half = lane < SSM_HEAD_DIM

    def expand(col_src):
        parts = []
        for p in range(SSM_HEADS // 2):
            lo = jnp.broadcast_to(col_src[:, 2 * p:2 * p + 1], (lc, LANES))
            hi = jnp.broadcast_to(col_src[:, 2 * p + 1:2 * p + 2], (lc, LANES))
            parts.append(jnp.where(low_half, lo, hi))
        return jnp.concatenate(parts, axis=1)

    dt_x = expand(dt)
    acs_x = expand(a_cs)
    xdt = xs * dt_x
    last_x = acs_x[lc - 1:lc, :]
    decay_to_end = jnp.exp(last_x - acs_x)
    decay_from_start = jnp.exp(acs_x)
    chunk_decay = jnp.exp(last_x)

    bm16 = bm.astype(BF16)
    cm16 = cm.astype(BF16)
    cb = []
    for g in range(SSM_GROUPS):
        cg = jnp.where(low_half if g == 0 else ~low_half, cm16, jnp.zeros_like(cm16))
        cb.append(lax.dot_general(cg, bm16, (((1,), (1,)), ((), ())), preferred_element_type=F32))
    xdt16 = xdt.astype(BF16)
    y_pairs = []
    for p in range(SSM_HEADS // 2):
        xp = xdt16[:, p * LANES:(p + 1) * LANES]
        yp = None
        for half in range(2):
            hd = 2 * p + half
            g = hd // (SSM_HEADS // SSM_GROUPS)
            diff = a_cs[:, hd:hd + 1] - a_cs_t[hd:hd + 1, :]
            dec = jnp.exp(jnp.where(lower, diff, -jnp.inf))
            scores = (cb[g] * dec).astype(BF16)
            xh = jnp.where(low_half if half == 0 else ~low_half, xp, jnp.zeros_like(xp))
            t = jnp.dot(scores, xh, preferred_element_type=F32)
            yp = t if yp is None else yp + t
        y_pairs.append(yp)
    y = jnp.concatenate(y_pairs, axis=1)

    prev = state[...]
    y = y + jnp.dot(cm16, prev.astype(BF16), preferred_element_type=F32) * decay_from_start
    xw = (xdt * decay_to_end).astype(BF16)
    cs = lax.dot_general(bm16, xw, (((0,), (0,)), ((), ())), preferred_element_type=F32)
    srow = lax.broadcasted_iota(jnp.int32, cs.shape, 0) // SSM_STATE
    scol = lax.broadcasted_iota(jnp.int32, cs.shape, 1) // (SSM_INNER // SSM_GROUPS)
    state[...] = prev * chunk_decay + jnp.where(srow == scol, cs, 0.0)

    y = y + xs * dskip_ref[...]
    zf = z_ref[...].astype(F32)
    y = y * (zf * _sigmoid(zf))
    gw = SSM_INNER // SSM_GROUPS
    outs = []
    for g in range(SSM_GROUPS):
        yg = y[:, g * gw:(g + 1) * gw]
        ms = jnp.mean(yg * yg, axis=-1, keepdims=True)
        outs.append(yg * lax.rsqrt(ms + NORM_EPS))
    y_ref[...] = (jnp.concatenate(outs, axis=1) * nw_ref[...]).astype(BF16)


def _ssd(z, xbc, dt, conv_w, conv_b, dt_bias, a_log, d_skip, norm_w, bsz, seq_len):
    n = z.shape[0]
    lc = SSM_CHUNK
    chunks = seq_len // lc
    row = lambda b, c: (b * chunks + c, 0)
    const = lambda b, c: (0, 0)
    return pl.pallas_call(
        _ssd_kernel,
        grid=(bsz, chunks),
        in_specs=[
            pl.BlockSpec((lc, SSM_INNER), row),
            pl.BlockSpec((lc, SSM_CONV_DIM), row),
            pl.BlockSpec((lc, LANES), row),
            pl.BlockSpec((SSM_CONV, SSM_CONV_DIM), const),
            pl.BlockSpec((1, SSM_CONV_DIM), const),
            pl.BlockSpec((1, LANES), const),
            pl.BlockSpec((1, LANES), const),
            pl.BlockSpec((1, SSM_INNER), const),
            pl.BlockSpec((1, SSM_INNER), const),
        ],
        out_specs=pl.BlockSpec((lc, SSM_INNER), row),
        out_shape=jax.ShapeDtypeStruct((n, SSM_INNER), BF16),
        scratch_shapes=[pltpu.VMEM((lc + HALO, SSM_CONV_DIM), F32),
                        pltpu.VMEM((SSM_GROUPS * SSM_STATE, SSM_INNER), F32)],
        compiler_params=_compiler_params(2),
        name="ssd_mixer",
    )(z, xbc, dt, conv_w, conv_b, dt_bias, a_log, d_skip, norm_w)


def _merge_kernel(x_ref, mod_ref, gpre_ref, gpost_ref, fa_ref, fb_ref, fc_ref,
                  wg_ref, wa_ref, wb_ref, wc_ref, wo_ref, o_ref):
    d = D_MODEL
    x = x_ref[...]
    shift = mod_ref[0, 0:1, :]
    scale = mod_ref[0, 1:2, :]
    gate = mod_ref[0, 2:3, :]
    h = _modulated_prenorm(x, gpre_ref[...], scale, shift).astype(BF16)
    merged = None
    for br, (f_ref, w_ref) in enumerate(((fa_ref, wa_ref), (fb_ref, wb_ref), (fc_ref, wc_ref))):
        gl = jnp.dot(h, wg_ref[:, br * d:(br + 1) * d], preferred_element_type=F32)
        yb = jnp.dot(f_ref[...], w_ref[...], preferred_element_type=F32)
        t = _sigmoid(gl) * yb
        merged = t if merged is None else merged + t
    mix = jnp.dot(merged.astype(BF16), wo_ref[...], preferred_element_type=F32)
    ms = jnp.mean(mix * mix, axis=-1, keepdims=True)
    o_ref[...] = x + gate * (mix * lax.rsqrt(ms + NORM_EPS) * gpost_ref[...])


def _merge(xf, mod_l, g_pre, g_post, fa, fb, fc, wg, wa, wb, wc, wo, seq_len):
    n, d = xf.shape
    tm = min(TOKEN_TILE, seq_len)
    tiles_per_seq = seq_len // tm
    row = lambda i: (i, 0)
    const = lambda i: (0, 0)
    return pl.pallas_call(
        _merge_kernel,
        grid=(n // tm,),
        in_specs=[
            pl.BlockSpec((tm, d), row),
            pl.BlockSpec((1, N_MOD, d), lambda i: (i // tiles_per_seq, 0, 0)),
            pl.BlockSpec((1, d), const),
            pl.BlockSpec((1, d), const),
            pl.BlockSpec((tm, SC_WIDTH), row),
            pl.BlockSpec((tm, SB_WIDTH), row),
            pl.BlockSpec((tm, SSM_INNER), row),
            pl.BlockSpec((d, 3 * d), const),
            pl.BlockSpec((SC_WIDTH, d), const),
            pl.BlockSpec((SB_WIDTH, d), const),
            pl.BlockSpec((SSM_INNER, d), const),
            pl.BlockSpec((d, d), const),
        ],
        out_specs=pl.BlockSpec((tm, d), row),
        out_shape=jax.ShapeDtypeStruct((n, d), F32),
        compiler_params=_compiler_params(1),
        name="merge_outproj",
    )(xf, mod_l, g_pre, g_post, fa, fb, fc, wg, wa, wb, wc, wo)


def _ffn_kernel(x_ref, mod_ref, gpre_ref, gpost_ref, win_ref, wout_ref, o_ref):
    x = x_ref[...]
    shift = mod_ref[0, 3:4, :]
    scale = mod_ref[0, 4:5, :]
    gate = mod_ref[0, 5:6, :]
    h = _modulated_prenorm(x, gpre_ref[...], scale, shift).astype(BF16)
    gt = jnp.dot(h, win_ref[:, :FFN_HIDDEN], preferred_element_type=F32)
    up = jnp.dot(h, win_ref[:, FFN_HIDDEN:], preferred_element_type=F32)
    act = (gt * _sigmoid(gt) * up).astype(BF16)
    f = jnp.dot(act, wout_ref[...], preferred_element_type=F32)
    ms = jnp.mean(f * f, axis=-1, keepdims=True)
    o_ref[...] = x + gate * (f * lax.rsqrt(ms + NORM_EPS) * gpost_ref[...])


def _ffn(xf, mod_l, g_pre, g_post, w_in, w_out, seq_len):
    n, d = xf.shape
    tm = min(TOKEN_TILE, seq_len)
    tiles_per_seq = seq_len // tm
    row = lambda i: (i, 0)
    const = lambda i: (0, 0)
    return pl.pallas_call(
        _ffn_kernel,
        grid=(n // tm,),
        in_specs=[
            pl.BlockSpec((tm, d), row),
            pl.BlockSpec((1, N_MOD, d), lambda i: (i // tiles_per_seq, 0, 0)),
            pl.BlockSpec((1, d), const),
            pl.BlockSpec((1, d), const),
            pl.BlockSpec((d, 2 * FFN_HIDDEN), const),
            pl.BlockSpec((FFN_HIDDEN, d), const),
        ],
        out_specs=pl.BlockSpec((tm, d), row),
        out_shape=jax.ShapeDtypeStruct((n, d), F32),
        compiler_params=_compiler_params(1),
        name="swiglu_ffn",
    )(xf, mod_l, g_pre, g_post, w_in, w_out)


def _pad_lanes(v):
    return jnp.pad(v.astype(F32), (0, LANES - v.shape[0])).reshape(1, LANES)


def kernel(x, c, mod_w, mod_b, g_pre_mix, g_post_mix, g_pre_ffn, g_post_ffn, w_in, sc_conv_w,
           ssm_conv_w, ssm_conv_b, ssm_dt_bias, ssm_a_log, ssm_d, ssm_norm_w, w_sc_out, w_sb_out,
           w_ssm_out, w_o, w_ffn_in, w_ffn_out):
    bsz, seq_len, d = x.shape
    depth = mod_w.shape[0]
    assert d == D_MODEL and seq_len % (ATTN_Q_BLOCKS * SB_BLOCK) == 0 and seq_len % SSM_CHUNK == 0
    mod = _modulation(c, mod_w, mod_b).reshape(depth, bsz, N_MOD, d)
    xf = x.reshape(bsz * seq_len, d)
    q_scale = SB_HEAD_DIM ** -0.5 * LOG2_E
    for l in range(depth):
        wl = w_in[l]
        w1 = jnp.concatenate([
            wl[:, :_OFF_QKV],
            wl[:, _OFF_QKV:_OFF_QKV + SB_WIDTH] * q_scale,
            wl[:, _OFF_QKV + SB_WIDTH:_OFF_DT],
            jnp.pad(wl[:, _OFF_DT:_OFF_DT + SSM_HEADS], ((0, 0), (0, LANES - SSM_HEADS))),
        ], axis=1).astype(BF16)
        wg = wl[:, _OFF_DT + SSM_HEADS:].astype(BF16)
        row = lambda v: v.reshape(1, -1).astype(F32)

        fa, qkv, z, xbc, dt = _inproj(xf, mod[l], row(g_pre_mix[l]), w1, sc_conv_w[l].astype(F32), seq_len)
        fb = _attention(qkv, bsz, seq_len)
        fc = _ssd(z, xbc, dt, ssm_conv_w[l].astype(F32), row(ssm_conv_b[l]), _pad_lanes(ssm_dt_bias[l]),
                  _pad_lanes(ssm_a_log[l]),
                  row(jnp.repeat(ssm_d[l], SSM_HEAD_DIM)), row(ssm_norm_w[l]), bsz, seq_len)
        xf = _merge(xf, mod[l], row(g_pre_mix[l]), row(g_post_mix[l]), fa, fb, fc, wg,
                    w_sc_out[l].astype(BF16), w_sb_out[l].astype(BF16), w_ssm_out[l].astype(BF16),
                    w_o[l].astype(BF16), seq_len)
        xf = _ffn(xf, mod[l], row(g_pre_ffn[l]), row(g_post_ffn[l]), w_ffn_in[l].astype(BF16),
                  w_ffn_out[l].astype(BF16), seq_len)
    return xf.reshape(bsz, seq_len, d)
```

```python
import functools

import jax
import jax.numpy as jnp
from jax import lax
from jax.experimental import pallas as pl
from jax.experimental.pallas import tpu as pltpu

D_MODEL = 1024
SC_WIDTH = 256
SC_KERNEL = 3
SB_HEAD_DIM = 64
SB_WIDTH = 256
SB_BLOCK = 128
SSM_INNER = 512
SSM_HEAD_DIM = 64
SSM_HEADS = 8
SSM_GROUPS = 2
SSM_STATE = 64
SSM_CONV = 4
SSM_CHUNK = 256
SSM_CONV_DIM = SSM_INNER + 2 * SSM_GROUPS * SSM_STATE
FFN_HIDDEN = 2816
NORM_EPS = 1e-6
LOG2_E = 1.4426950408889634
N_MOD = 6

LANES = 128
SUBLANES = 8
VMEM_LIMIT_BYTES = 56 * 1024 * 1024

TOKEN_TILE = 512
ATTN_Q_BLOCKS = 8
ATTN_UNROLL = 8
HALO = SUBLANES

F32 = jnp.float32
BF16 = jnp.bfloat16


def _compiler_params(n_axes):
    return pltpu.CompilerParams(
        dimension_semantics=("arbitrary",) * n_axes,
        vmem_limit_bytes=VMEM_LIMIT_BYTES,
    )


def _sigmoid(v):
    return 1.0 / (1.0 + jnp.exp(-v))


def _softplus(v):
    return jnp.maximum(v, 0.0) + jnp.log(1.0 + jnp.exp(-jnp.abs(v)))


def _modulated_prenorm(x, g, scale, shift):
    ms = jnp.mean(x * x, axis=-1, keepdims=True)
    return x * lax.rsqrt(ms + NORM_EPS) * (g * (1.0 + scale)) + shift


def _mod_kernel(c_ref, w_ref, b_ref, o_ref):
    c = c_ref[...]
    sc = c * _sigmoid(c)
    o_ref[0] = jnp.dot(sc, w_ref[0], preferred_element_type=F32,
                       precision=lax.Precision.HIGHEST) + b_ref[0]


def _modulation(c, mod_w, mod_b):
    depth, d, n_out = mod_w.shape
    bsz = c.shape[0]
    tn = D_MODEL
    return pl.pallas_call(
        _mod_kernel,
        grid=(depth, n_out // tn),
        in_specs=[
            pl.BlockSpec((bsz, d), lambda l, j: (0, 0)),
            pl.BlockSpec((1, d, tn), lambda l, j: (l, 0, j)),
            pl.BlockSpec((1, 1, tn), lambda l, j: (l, 0, j)),
        ],
        out_specs=pl.BlockSpec((1, bsz, tn), lambda l, j: (l, 0, j)),
        out_shape=jax.ShapeDtypeStruct((depth, bsz, n_out), F32),
        compiler_params=_compiler_params(2),
        name="adaln_modulation",
    )(c, mod_w, mod_b.reshape(depth, 1, n_out))


_A_COLS = 3 * SC_WIDTH
_QKV_COLS = 3 * SB_WIDTH
_OFF_QKV = _A_COLS
_OFF_Z = _OFF_QKV + _QKV_COLS
_OFF_XBC = _OFF_Z + SSM_INNER
_OFF_DT = _OFF_XBC + SSM_CONV_DIM
_W1_COLS = _OFF_DT + LANES


def _inproj_kernel(tiles_per_seq, x_ref, mod_ref, g_ref, w_ref, cw_ref,
                   fa_ref, qkv_ref, z_ref, xbc_ref, dt_ref, ubuf):
    i = pl.program_id(0)
    tm = x_ref.shape[0]
    shift = mod_ref[0, 0:1, :]
    scale = mod_ref[0, 1:2, :]
    h = _modulated_prenorm(x_ref[...], g_ref[...], scale, shift).astype(BF16)

    def proj(lo, hi):
        return jnp.dot(h, w_ref[:, lo:hi], preferred_element_type=F32)

    qkv_ref[...] = proj(_OFF_QKV, _OFF_Z).astype(BF16)
    z_ref[...] = proj(_OFF_Z, _OFF_XBC).astype(BF16)
    xbc_ref[...] = proj(_OFF_XBC, _OFF_DT).astype(BF16)
    dt_ref[...] = proj(_OFF_DT, _W1_COLS)

    pa = proj(0, _A_COLS)
    u = pa[:, SC_WIDTH:2 * SC_WIDTH] * pa[:, 2 * SC_WIDTH:]

    @pl.when(i % tiles_per_seq == 0)
    def _():
        ubuf[0:HALO, :] = jnp.zeros((HALO, SC_WIDTH), F32)

    @pl.when(i % tiles_per_seq != 0)
    def _():
        ubuf[0:HALO, :] = ubuf[tm:tm + HALO, :]

    ubuf[HALO:, :] = u
    conv = cw_ref[SC_KERNEL - 1:SC_KERNEL, :] * u
    for k in range(SC_KERNEL - 1):
        back = SC_KERNEL - 1 - k
        conv = conv + cw_ref[k:k + 1, :] * ubuf[HALO - back:HALO - back + tm, :]
    fa_ref[...] = (pa[:, :SC_WIDTH] * conv).astype(BF16)


def _inproj(xf, mod_l, g_pre, w1, sc_conv_w, seq_len):
    n, d = xf.shape
    tm = min(TOKEN_TILE, seq_len)
    tiles_per_seq = seq_len // tm
    row = lambda i: (i, 0)
    const = lambda i: (0, 0)
    return pl.pallas_call(
        functools.partial(_inproj_kernel, tiles_per_seq),
        grid=(n // tm,),
        in_specs=[
            pl.BlockSpec((tm, d), row),
            pl.BlockSpec((1, N_MOD, d), lambda i: (i // tiles_per_seq, 0, 0)),
            pl.BlockSpec((1, d), const),
            pl.BlockSpec((d, _W1_COLS), const),
            pl.BlockSpec((SC_KERNEL, SC_WIDTH), const),
        ],
        out_specs=[
            pl.BlockSpec((tm, SC_WIDTH), row),
            pl.BlockSpec((tm, _QKV_COLS), row),
            pl.BlockSpec((tm, SSM_INNER), row),
            pl.BlockSpec((tm, SSM_CONV_DIM), row),
            pl.BlockSpec((tm, LANES), row),
        ],
        out_shape=[
            jax.ShapeDtypeStruct((n, SC_WIDTH), BF16),
            jax.ShapeDtypeStruct((n, _QKV_COLS), BF16),
            jax.ShapeDtypeStruct((n, SSM_INNER), BF16),
            jax.ShapeDtypeStruct((n, SSM_CONV_DIM), BF16),
            jax.ShapeDtypeStruct((n, LANES), F32),
        ],
        scratch_shapes=[pltpu.VMEM((tm + HALO, SC_WIDTH), F32)],
        compiler_params=_compiler_params(1),
        name="inproj_shortconv",
    )(xf, mod_l, g_pre, w1, sc_conv_w)


def _attn_kernel(q_ref, k_ref, v_ref, o_ref, acc, carry):
    blk = SB_BLOCK
    rows = q_ref.shape[0]
    first_blk = pl.program_id(2) * ATTN_Q_BLOCKS
    q = q_ref[...]

    lane_k = lax.broadcasted_iota(jnp.int32, (blk, LANES), 1)
    low_half = lane_k < SB_HEAD_DIM
    jj = lax.broadcasted_iota(jnp.int32, (2 * blk, 2 * blk), 0)
    ss = lax.broadcasted_iota(jnp.int32, (2 * blk, 2 * blk), 1)
    same_head = (jj // blk) == (ss // blk)
    later_mat = jnp.where(same_head & (jj > ss), 1.0, 0.0).astype(BF16)
    total_mat = jnp.where(same_head, 1.0, 0.0).astype(BF16)
    diag_row = lax.broadcasted_iota(jnp.int32, (blk, 2 * blk), 0)
    diag_key = lax.broadcasted_iota(jnp.int32, (blk, 2 * blk), 1) % blk
    causal = diag_key < diag_row

    def stacked(ref, j):
        t = ref[pl.ds(pl.multiple_of(j * blk, blk), blk), :]
        zero = jnp.zeros_like(t)
        return jnp.concatenate([jnp.where(low_half, t, zero), jnp.where(low_half, zero, t)], axis=0)

    def mask_diagonal(t):
        top = jnp.where(causal, t[:blk], 0.0)
        return top if t.shape[0] == blk else jnp.concatenate([top, t[blk:]], axis=0)

    def step(j, r0, diagonal):
        z = lax.dot_general(q[r0:], stacked(k_ref, j), (((1,), (1,)), ((), ())), preferred_element_type=F32)
        neg_abs = pltpu.bitcast(pltpu.bitcast(z, jnp.uint32) | jnp.uint32(0x80000000), F32)
        sp = jnp.maximum(z, 0.0) + jnp.log2(1.0 + jnp.exp2(neg_abs))
        log_beta = z - sp
        if diagonal:
            sp = mask_diagonal(sp)
        sp16 = sp.astype(BF16)
        wp = jnp.dot(sp16, later_mat, preferred_element_type=F32)
        tp = jnp.dot(sp16, total_mat, preferred_element_type=F32)
        c = carry[r0:, :]
        att = jnp.exp2(log_beta - (wp + c))
        if diagonal:
            att = mask_diagonal(att)
        carry[r0:, :] = c + tp
        acc[r0:, :] += jnp.dot(att.astype(BF16), stacked(v_ref, j), preferred_element_type=F32)

    carry[...] = jnp.zeros_like(carry)
    acc[...] = jnp.zeros_like(acc)
    for kb in range(ATTN_Q_BLOCKS - 1, -1, -1):
        step(first_blk + kb, kb * blk, True)

    def body(t, _):
        for u in range(ATTN_UNROLL):
            step(first_blk - 1 - (t * ATTN_UNROLL + u), 0, False)
        return 0

    lax.fori_loop(0, first_blk // ATTN_UNROLL, body, 0)
    o_ref[...] = acc[...].astype(BF16)


def _attention(qkv, bsz, seq_len):
    n = qkv.shape[0]
    rows = ATTN_Q_BLOCKS * SB_BLOCK
    steps = seq_len // rows
    pairs = SB_WIDTH // LANES
    return pl.pallas_call(
        _attn_kernel,
        grid=(bsz, pairs, steps),
        in_specs=[
            pl.BlockSpec((rows, LANES), lambda b, p, i: (b * steps + i, p)),
            pl.BlockSpec((seq_len, LANES), lambda b, p, i: (b, pairs + p)),
            pl.BlockSpec((seq_len, LANES), lambda b, p, i: (b, 2 * pairs + p)),
        ],
        out_specs=pl.BlockSpec((rows, LANES), lambda b, p, i: (b * steps + i, p)),
        out_shape=jax.ShapeDtypeStruct((n, SB_WIDTH), BF16),
        scratch_shapes=[pltpu.VMEM((rows, LANES), F32), pltpu.VMEM((rows, 2 * SB_BLOCK), F32)],
        compiler_params=_compiler_params(3),
        name="stickbreak_attention",
    )(qkv, qkv, qkv)


def _ssd_kernel(z_ref, xbc_ref, dt_ref, cw_ref, cb_ref, dtb_ref, alog_ref, dskip_ref, nw_ref,
                y_ref, xbuf, state):
    lc = SSM_CHUNK
    c = pl.program_id(1)

    @pl.when(c == 0)
    def _():
        xbuf[0:HALO, :] = jnp.zeros((HALO, SSM_CONV_DIM), F32)
        state[...] = jnp.zeros_like(state)

    @pl.when(c != 0)
    def _():
        xbuf[0:HALO, :] = xbuf[lc:lc + HALO, :]

    xbuf[HALO:, :] = xbc_ref[...].astype(F32)
    conv = cb_ref[...] + cw_ref[SSM_CONV - 1:SSM_CONV, :] * xbuf[HALO:, :]
    for k in range(SSM_CONV - 1):
        back = SSM_CONV - 1 - k
        conv = conv + cw_ref[k:k + 1, :] * xbuf[HALO - back:HALO - back + lc, :]
    xc = conv * _sigmoid(conv)
    xs = xc[:, :SSM_INNER]
    bm = xc[:, SSM_INNER:SSM_INNER + LANES]
    cm = xc[:, SSM_INNER + LANES:]

    dt = _softplus(dt_ref[...] + dtb_ref[...])
    a = dt * -jnp.exp(alog_ref[...])
    ri = lax.broadcasted_iota(jnp.int32, (lc, lc), 0)
    ci = lax.broadcasted_iota(jnp.int32, (lc, lc), 1)
    lower = ri >= ci
    a_cs = jnp.dot(jnp.where(lower, 1.0, 0.0), a, preferred_element_type=F32,
                   precision=lax.Precision.HIGHEST)
    a_cs_t = a_cs.T

    lane = lax.broadcasted_iota(jnp.int32, (lc, LANES), 1)
    low_half = lane < SSM_HEAD_DIM

    def expand(col_src):
        parts = []
        for p in range(SSM_HEADS // 2):
            lo = jnp.broadcast_to(col_src[:, 2 * p:2 * p + 1], (lc, LANES))
            hi = jnp.broadcast_to(col_src[:, 2 * p + 1:2 * p + 2], (lc, LANES))
            parts.append(jnp.where(low_half, lo, hi))
        return jnp.concatenate(parts, axis=1)

    dt_x = expand(dt)
    acs_x = expand(a_cs)
    xdt = xs * dt_x
    last_x = acs_x[lc - 1:lc, :]
    decay_to_end = jnp.exp(last_x - acs_x)
    decay_from_start = jnp.exp(acs_x)
    chunk_decay = jnp.exp(last_x)

    bm16 = bm.astype(BF16)
    cm16 = cm.astype(BF16)
    cb = []
    for g in range(SSM_GROUPS):
        cg = jnp.where(low_half if g == 0 else ~low_half, cm16, jnp.zeros_like(cm16))
        cb.append(lax.dot_general(cg, bm16, (((1,), (1,)), ((), ())), preferred_element_type=F32))
    xdt16 = xdt.astype(BF16)
    y_pairs = []
    for p in range(SSM_HEADS // 2):
        xp = xdt16[:, p * LANES:(p + 1) * LANES]
        yp = None
        for half in range(2):
            hd = 2 * p + half
            g = hd // (SSM_HEADS // SSM_GROUPS)
            diff = a_cs[:, hd:hd + 1] - a_cs_t[hd:hd + 1, :]
            dec = jnp.exp(jnp.where(lower, diff, -jnp.inf))
            scores = (cb[g] * dec).astype(BF16)
            xh = jnp.where(low_half if half == 0 else ~low_half, xp, jnp.zeros_like(xp))
            t = jnp.dot(scores, xh, preferred_element_type=F32)
            yp = t if yp is None else yp + t
        y_pairs.append(yp)
    y = jnp.concatenate(y_pairs, axis=1)

    prev = state[...]
    y = y + jnp.dot(cm16, prev.astype(BF16), preferred_element_type=F32) * decay_from_start
    xw = (xdt * decay_to_end).astype(BF16)
    cs = lax.dot_general(bm16, xw, (((0,), (0,)), ((), ())), preferred_element_type=F32)
    srow = lax.broadcasted_iota(jnp.int32, cs.shape, 0) // SSM_STATE
    scol = lax.broadcasted_iota(jnp.int32, cs.shape, 1) // (SSM_INNER // SSM_GROUPS)
    state[...] = prev * chunk_decay + jnp.where(srow == scol, cs, 0.0)

    y = y + xs * dskip_ref[...]
    zf = z_ref[...].astype(F32)
    y = y * (zf * _sigmoid(zf))
    gw = SSM_INNER // SSM_GROUPS
    outs = []
    for g in range(SSM_GROUPS):
        yg = y[:, g * gw:(g + 1) * gw]
        ms = jnp.mean(yg * yg, axis=-1, keepdims=True)
        outs.append(yg * lax.rsqrt(ms + NORM_EPS))
    y_ref[...] = (jnp.concatenate(outs, axis=1) * nw_ref[...]).astype(BF16)


def _ssd(z, xbc, dt, conv_w, conv_b, dt_bias, a_log, d_skip, norm_w, bsz, seq_len):
    n = z.shape[0]
    lc = SSM_CHUNK
    chunks = seq_len // lc
    row = lambda b, c: (b * chunks + c, 0)
    const = lambda b, c: (0, 0)
    return pl.pallas_call(
        _ssd_kernel,
        grid=(bsz, chunks),
        in_specs=[
            pl.BlockSpec((lc, SSM_INNER), row),
            pl.BlockSpec((lc, SSM_CONV_DIM), row),
            pl.BlockSpec((lc, LANES), row),
            pl.BlockSpec((SSM_CONV, SSM_CONV_DIM), const),
            pl.BlockSpec((1, SSM_CONV_DIM), const),
            pl.BlockSpec((1, LANES), const),
            pl.BlockSpec((1, LANES), const),
            pl.BlockSpec((1, SSM_INNER), const),
            pl.BlockSpec((1, SSM_INNER), const),
        ],
        out_specs=pl.BlockSpec((lc, SSM_INNER), row),
        out_shape=jax.ShapeDtypeStruct((n, SSM_INNER), BF16),
        scratch_shapes=[pltpu.VMEM((lc + HALO, SSM_CONV_DIM), F32),
                        pltpu.VMEM((SSM_GROUPS * SSM_STATE, SSM_INNER), F32)],
        compiler_params=_compiler_params(2),
        name="ssd_mixer",
    )(z, xbc, dt, conv_w, conv_b, dt_bias, a_log, d_skip, norm_w)


def _merge_kernel(x_ref, mod_ref, gpre_ref, gpost_ref, fa_ref, fb_ref, fc_ref,
                  wg_ref, wa_ref, wb_ref, wc_ref, wo_ref, o_ref):
    d = D_MODEL
    x = x_ref[...]
    shift = mod_ref[0, 0:1, :]
    scale = mod_ref[0, 1:2, :]
    gate = mod_ref[0, 2:3, :]
    h = _modulated_prenorm(x, gpre_ref[...], scale, shift).astype(BF16)
    merged = None
    for br, (f_ref, w_ref) in enumerate(((fa_ref, wa_ref), (fb_ref, wb_ref), (fc_ref, wc_ref))):
        gl = jnp.dot(h, wg_ref[:, br * d:(br + 1) * d], preferred_element_type=F32)
        yb = jnp.dot(f_ref[...], w_ref[...], preferred_element_type=F32)
        t = _sigmoid(gl) * yb
        merged = t if merged is None else merged + t
    mix = jnp.dot(merged.astype(BF16), wo_ref[...], preferred_element_type=F32)
    ms = jnp.mean(mix * mix, axis=-1, keepdims=True)
    o_ref[...] = x + gate * (mix * lax.rsqrt(ms + NORM_EPS) * gpost_ref[...])


def _merge(xf, mod_l, g_pre, g_post, fa, fb, fc, wg, wa, wb, wc, wo, seq_len):
    n, d = xf.shape
    tm = min(TOKEN_TILE, seq_len)
    tiles_per_seq = seq_len // tm
    row = lambda i: (i, 0)
    const = lambda i: (0, 0)
    return pl.pallas_call(
        _merge_kernel,
        grid=(n // tm,),
        in_specs=[
            pl.BlockSpec((tm, d), row),
            pl.BlockSpec((1, N_MOD, d), lambda i: (i // tiles_per_seq, 0, 0)),
            pl.BlockSpec((1, d), const),
            pl.BlockSpec((1, d), const),
            pl.BlockSpec((tm, SC_WIDTH), row),
            pl.BlockSpec((tm, SB_WIDTH), row),
            pl.BlockSpec((tm, SSM_INNER), row),
            pl.BlockSpec((d, 3 * d), const),
            pl.BlockSpec((SC_WIDTH, d), const),
            pl.BlockSpec((SB_WIDTH, d), const),
            pl.BlockSpec((SSM_INNER, d), const),
            pl.BlockSpec((d, d), const),
        ],
        out_specs=pl.BlockSpec((tm, d), row),
        out_shape=jax.ShapeDtypeStruct((n, d), F32),
        compiler_params=_compiler_params(1),
        name="merge_outproj",
    )(xf, mod_l, g_pre, g_post, fa, fb, fc, wg, wa, wb, wc, wo)


def _ffn_kernel(x_ref, mod_ref, gpre_ref, gpost_ref, win_ref, wout_ref, o_ref):
    x = x_ref[...]
    shift = mod_ref[0, 3:4, :]
    scale = mod_ref[0, 4:5, :]
    gate = mod_ref[0, 5:6, :]
    h = _modulated_prenorm(x, gpre_ref[...], scale, shift).astype(BF16)
    gt = jnp.dot(h, win_ref[:, :FFN_HIDDEN], preferred_element_type=F32)
    up = jnp.dot(h, win_ref[:, FFN_HIDDEN:], preferred_element_type=F32)
    act = (gt * _sigmoid(gt) * up).astype(BF16)
    f = jnp.dot(act, wout_ref[...], preferred_element_type=F32)
    ms = jnp.mean(f * f, axis=-1, keepdims=True)
    o_ref[...] = x + gate * (f * lax.rsqrt(ms + NORM_EPS) * gpost_ref[...])


def _ffn(xf, mod_l, g_pre, g_post, w_in, w_out, seq_len):
    n, d = xf.shape
    tm = min(TOKEN_TILE, seq_len)
    tiles_per_seq = seq_len // tm
    row = lambda i: (i, 0)
    const = lambda i: (0, 0)
    return pl.pallas_call(
        _ffn_kernel,
        grid=(n // tm,),
        in_specs=[
            pl.BlockSpec((tm, d), row),
            pl.BlockSpec((1, N_MOD, d), lambda i: (i // tiles_per_seq, 0, 0)),
            pl.BlockSpec((1, d), const),
            pl.BlockSpec((1, d), const),
            pl.BlockSpec((d, 2 * FFN_HIDDEN), const),
            pl.BlockSpec((FFN_HIDDEN, d), const),
        ],
        out_specs=pl.BlockSpec((tm, d), row),
        out_shape=jax.ShapeDtypeStruct((n, d), F32),
        compiler_params=_compiler_params(1),
        name="swiglu_ffn",
    )(xf, mod_l, g_pre, g_post, w_in, w_out)


def _pad_lanes(v):
    return jnp.pad(v.astype(F32), (0, LANES - v.shape[0])).reshape(1, LANES)


def kernel(x, c, mod_w, mod_b, g_pre_mix, g_post_mix, g_pre_ffn, g_post_ffn, w_in, sc_conv_w,
           ssm_conv_w, ssm_conv_b, ssm_dt_bias, ssm_a_log, ssm_d, ssm_norm_w, w_sc_out, w_sb_out,
           w_ssm_out, w_o, w_ffn_in, w_ffn_out):
    bsz, seq_len, d = x.shape
    depth = mod_w.shape[0]
    assert d == D_MODEL and seq_len % (ATTN_Q_BLOCKS * SB_BLOCK) == 0 and seq_len % SSM_CHUNK == 0
    mod = _modulation(c, mod_w, mod_b).reshape(depth, bsz, N_MOD, d)
    xf = x.reshape(bsz * seq_len, d)
    q_scale = SB_HEAD_DIM ** -0.5 * LOG2_E
    for l in range(depth):
        wl = w_in[l]
        w1 = jnp.concatenate([
            wl[:, :_OFF_QKV],
            wl[:, _OFF_QKV:_OFF_QKV + SB_WIDTH] * q_scale,
            wl[:, _OFF_QKV + SB_WIDTH:_OFF_DT],
            jnp.pad(wl[:, _OFF_DT:_OFF_DT + SSM_HEADS], ((0, 0), (0, LANES - SSM_HEADS))),
        ], axis=1).astype(BF16)
        wg = wl[:, _OFF_DT + SSM_HEADS:].astype(BF16)
        row = lambda v: v.reshape(1, -1).astype(F32)

        fa, qkv, z, xbc, dt = _inproj(xf, mod[l], row(g_pre_mix[l]), w1, sc_conv_w[l].astype(F32), seq_len)
        fb = _attention(qkv, bsz, seq_len)
        fc = _ssd(z, xbc, dt, ssm_conv_w[l].astype(F32), row(ssm_conv_b[l]), _pad_lanes(ssm_dt_bias[l]),
                  _pad_lanes(ssm_a_log[l]),
                  row(jnp.repeat(ssm_d[l], SSM_HEAD_DIM)), row(ssm_norm_w[l]), bsz, seq_len)
        xf = _merge(xf, mod[l], row(g_pre_mix[l]), row(g_post_mix[l]), fa, fb, fc, wg,
                    w_sc_out[l].astype(BF16), w_sb_out[l].astype(BF16), w_ssm_out[l].astype(BF16),
                    w_o[l].astype(BF16), seq_len)
        xf = _ffn(xf, mod[l], row(g_pre_ffn[l]), row(g_post_ffn[l]), w_ffn_in[l].astype(BF16),
                  w_ffn_out[l].astype(BF16), seq_len)
    return xf.reshape(bsz, seq_len, d)
```

```python
import functools

import jax
import jax.numpy as jnp
from jax import lax
from jax.experimental import pallas as pl
from jax.experimental.pallas import tpu as pltpu

D_MODEL = 1024
SC_WIDTH = 256
SC_KERNEL = 3
SB_HEAD_DIM = 64
SB_WIDTH = 256
SB_BLOCK = 128
SSM_INNER = 512
SSM_HEAD_DIM = 64
SSM_HEADS = 8
SSM_GROUPS = 2
SSM_STATE = 64
SSM_CONV = 4
SSM_CHUNK = 256
SSM_CONV_DIM = SSM_INNER + 2 * SSM_GROUPS * SSM_STATE
FFN_HIDDEN = 2816
NORM_EPS = 1e-6
LOG2_E = 1.4426950408889634
N_MOD = 6

LANES = 128
SUBLANES = 8
VMEM_LIMIT_BYTES = 56 * 1024 * 1024

TOKEN_TILE = 1024
SUB_TILE = 512
ATTN_Q_BLOCKS = 8
ATTN_UNROLL = 8
SSD_CHUNKS_PER_STEP = 4
HALO = SUBLANES

F32 = jnp.float32
BF16 = jnp.bfloat16


def _compiler_params(n_axes):
    return pltpu.CompilerParams(
        dimension_semantics=("arbitrary",) * n_axes,
        vmem_limit_bytes=VMEM_LIMIT_BYTES,
    )


def _resident(block_shape):
    zeros = (0,) * len(block_shape)
    return pl.BlockSpec(block_shape, lambda *_: zeros, pipeline_mode=pl.Buffered(1))


def _sub_tiles(rows):
    sub = min(SUB_TILE, rows)
    assert rows % sub == 0
    return [slice(r, r + sub) for r in range(0, rows, sub)]


def _sigmoid(v):
    return 1.0 / (1.0 + jnp.exp(-v))


def _softplus(v):
    return jnp.maximum(v, 0.0) + jnp.log(1.0 + jnp.exp(-jnp.abs(v)))


def _modulated_prenorm(x, g, scale, shift):
    ms = jnp.mean(x * x, axis=-1, keepdims=True)
    return x * lax.rsqrt(ms + NORM_EPS) * (g * (1.0 + scale)) + shift


def _mod_kernel(c_ref, w_ref, b_ref, o_ref):
    c = c_ref[...]
    sc = c * _sigmoid(c)
    o_ref[0] = jnp.dot(sc, w_ref[0], preferred_element_type=F32,
                       precision=lax.Precision.HIGHEST) + b_ref[0]


def _modulation(c, mod_w, mod_b):
    depth, d, n_out = mod_w.shape
    bsz = c.shape[0]
    tn = D_MODEL
    return pl.pallas_call(
        _mod_kernel,
        grid=(depth, n_out // tn),
        in_specs=[
            pl.BlockSpec((bsz, d), lambda l, j: (0, 0)),
            pl.BlockSpec((1, d, tn), lambda l, j: (l, 0, j)),
            pl.BlockSpec((1, 1, tn), lambda l, j: (l, 0, j)),
        ],
        out_specs=pl.BlockSpec((1, bsz, tn), lambda l, j: (l, 0, j)),
        out_shape=jax.ShapeDtypeStruct((depth, bsz, n_out), F32),
        compiler_params=_compiler_params(2),
        name="adaln_modulation",
    )(c, mod_w, mod_b.reshape(depth, 1, n_out))


_A_COLS = 3 * SC_WIDTH
_QKV_COLS = 3 * SB_WIDTH
_OFF_QKV = _A_COLS
_OFF_Z = _OFF_QKV + _QKV_COLS
_OFF_XBC = _OFF_Z + SSM_INNER
_OFF_DT = _OFF_XBC + SSM_CONV_DIM
_W1_COLS = _OFF_DT + LANES


def _inproj_kernel(tiles_per_seq, x_ref, mod_ref, g_ref, w_ref, cw_ref,
                   fa_ref, qkv_ref, z_ref, xbc_ref, dt_ref, ubuf):
    i = pl.program_id(0)
    tm = x_ref.shape[0]
    shift = mod_ref[0, 0:1, :]
    scale = mod_ref[0, 1:2, :]

    @pl.when(i % tiles_per_seq == 0)
    def _():
        ubuf[0:HALO, :] = jnp.zeros((HALO, SC_WIDTH), F32)

    @pl.when(i % tiles_per_seq != 0)
    def _():
        ubuf[0:HALO, :] = ubuf[tm:tm + HALO, :]

    for rs in _sub_tiles(tm):
        h = _modulated_prenorm(x_ref[rs, :], g_ref[...], scale, shift).astype(BF16)

        def proj(lo, hi):
            return jnp.dot(h, w_ref[:, lo:hi], preferred_element_type=F32)

        qkv_ref[rs, :] = proj(_OFF_QKV, _OFF_Z).astype(BF16)
        z_ref[rs, :] = proj(_OFF_Z, _OFF_XBC).astype(BF16)
        xbc_ref[rs, :] = proj(_OFF_XBC, _OFF_DT).astype(BF16)
        dt_ref[rs, :] = proj(_OFF_DT, _W1_COLS)

        pa = proj(0, _A_COLS)
        u = pa[:, SC_WIDTH:2 * SC_WIDTH] * pa[:, 2 * SC_WIDTH:]
        ubuf[HALO + rs.start:HALO + rs.stop, :] = u
        conv = cw_ref[SC_KERNEL - 1:SC_KERNEL, :] * u
        for k in range(SC_KERNEL - 1):
            back = SC_KERNEL - 1 - k
            conv = conv + cw_ref[k:k + 1, :] * ubuf[HALO - back + rs.start:HALO - back + rs.stop, :]
        fa_ref[rs, :] = (pa[:, :SC_WIDTH] * conv).astype(BF16)


def _inproj(xf, mod_l, g_pre, w1, sc_conv_w, seq_len):
    n, d = xf.shape
    tm = min(TOKEN_TILE, seq_len)
    tiles_per_seq = seq_len // tm
    row = lambda i: (i, 0)
    return pl.pallas_call(
        functools.partial(_inproj_kernel, tiles_per_seq),
        grid=(n // tm,),
        in_specs=[
            pl.BlockSpec((tm, d), row),
            pl.BlockSpec((1, N_MOD, d), lambda i: (i // tiles_per_seq, 0, 0)),
            _resident((1, d)),
            _resident((d, _W1_COLS)),
            _resident((SC_KERNEL, SC_WIDTH)),
        ],
        out_specs=[
            pl.BlockSpec((tm, SC_WIDTH), row),
            pl.BlockSpec((tm, _QKV_COLS), row),
            pl.BlockSpec((tm, SSM_INNER), row),
            pl.BlockSpec((tm, SSM_CONV_DIM), row),
            pl.BlockSpec((tm, LANES), row),
        ],
        out_shape=[
            jax.ShapeDtypeStruct((n, SC_WIDTH), BF16),
            jax.ShapeDtypeStruct((n, _QKV_COLS), BF16),
            jax.ShapeDtypeStruct((n, SSM_INNER), BF16),
            jax.ShapeDtypeStruct((n, SSM_CONV_DIM), BF16),
            jax.ShapeDtypeStruct((n, LANES), F32),
        ],
        scratch_shapes=[pltpu.VMEM((tm + HALO, SC_WIDTH), F32)],
        compiler_params=_compiler_params(1),
        name="inproj_shortconv",
    )(xf, mod_l, g_pre, w1, sc_conv_w)


def _attn_kernel(q_ref, k_ref, v_ref, o_ref, acc, carry, lb_buf, sp_buf):
    blk = SB_BLOCK
    first_blk = pl.program_id(2) * ATTN_Q_BLOCKS
    q = q_ref[...]

    lane_k = lax.broadcasted_iota(jnp.int32, (blk, LANES), 1)
    low_half = lane_k < SB_HEAD_DIM
    jj = lax.broadcasted_iota(jnp.int32, (2 * blk, 2 * blk), 0)
    ss = lax.broadcasted_iota(jnp.int32, (2 * blk, 2 * blk), 1)
    same_head = (jj // blk) == (ss // blk)
    later_mat = jnp.where(same_head & (jj > ss), 1.0, 0.0).astype(BF16)
    total_mat = jnp.where(same_head, 1.0, 0.0).astype(BF16)
    diag_row = lax.broadcasted_iota(jnp.int32, (blk, 2 * blk), 0)
    diag_key = lax.broadcasted_iota(jnp.int32, (blk, 2 * blk), 1) % blk
    causal = diag_key < diag_row

    def stacked(ref, j):
        t = ref[pl.ds(pl.multiple_of(j * blk, blk), blk), :]
        zero = jnp.zeros_like(t)
        return jnp.concatenate([jnp.where(low_half, t, zero), jnp.where(low_half, zero, t)], axis=0)

    def mask_diagonal(t):
        top = jnp.where(causal, t[:blk], 0.0)
        return top if t.shape[0] == blk else jnp.concatenate([top, t[blk:]], axis=0)

    def softplus_terms(j, r0):
        z = lax.dot_general(q[r0:], stacked(k_ref, j), (((1,), (1,)), ((), ())), preferred_element_type=F32)
        neg_abs = pltpu.bitcast(pltpu.bitcast(z, jnp.uint32) | jnp.uint32(0x80000000), F32)
        sp = jnp.maximum(z, 0.0) + jnp.log2(1.0 + jnp.exp2(neg_abs))
        return z - sp, sp

    def attend(j, r0, log_beta, sp16, diagonal):
        wp = jnp.dot(sp16, later_mat, preferred_element_type=F32)
        tp = jnp.dot(sp16, total_mat, preferred_element_type=F32)
        c = carry[r0:, :]
        att = jnp.exp2(log_beta - (wp + c))
        if diagonal:
            att = mask_diagonal(att)
        carry[r0:, :] = c + tp
        acc[r0:, :] += jnp.dot(att.astype(BF16), stacked(v_ref, j), preferred_element_type=F32)

    def step(j, r0, diagonal):
        log_beta, sp = softplus_terms(j, r0)
        if diagonal:
            sp = mask_diagonal(sp)
        attend(j, r0, log_beta, sp.astype(BF16), diagonal)

    def stage(j):
        log_beta, sp = softplus_terms(j, 0)
        lb_buf[...] = log_beta
        sp_buf[...] = sp.astype(BF16)

    carry[...] = jnp.zeros_like(carry)
    acc[...] = jnp.zeros_like(acc)
    for kb in range(ATTN_Q_BLOCKS - 1, -1, -1):
        step(first_blk + kb, kb * blk, True)

    stage(jnp.maximum(first_blk - 1, 0))

    def body(t, _):
        j = first_blk - 1 - t * ATTN_UNROLL
        attend(j, 0, lb_buf[...], sp_buf[...], False)
        for u in range(1, ATTN_UNROLL):
            step(j - u, 0, False)
        stage(jnp.maximum(j - ATTN_UNROLL, 0))
        return 0

    lax.fori_loop(0, first_blk // ATTN_UNROLL, body, 0)
    o_ref[...] = acc[...].astype(BF16)


def _attention(qkv, bsz, seq_len):
    n = qkv.shape[0]
    rows = ATTN_Q_BLOCKS * SB_BLOCK
    steps = seq_len // rows
    pairs = SB_WIDTH // LANES
    return pl.pallas_call(
        _attn_kernel,
        grid=(bsz, pairs, steps),
        in_specs=[
            pl.BlockSpec((rows, LANES), lambda b, p, i: (b * steps + i, p)),
            pl.BlockSpec((seq_len, LANES), lambda b, p, i: (b, pairs + p)),
            pl.BlockSpec((seq_len, LANES), lambda b, p, i: (b, 2 * pairs + p)),
        ],
        out_specs=pl.BlockSpec((rows, LANES), lambda b, p, i: (b * steps + i, p)),
        out_shape=jax.ShapeDtypeStruct((n, SB_WIDTH), BF16),
        scratch_shapes=[pltpu.VMEM((rows, LANES), F32), pltpu.VMEM((rows, 2 * SB_BLOCK), F32),
                        pltpu.VMEM((rows, 2 * SB_BLOCK), F32), pltpu.VMEM((rows, 2 * SB_BLOCK), BF16)],
        compiler_params=_compiler_params(3),
        name="stickbreak_attention",
    )(qkv, qkv, qkv)


def _ssd_kernel(z_ref, xbc_ref, dt_ref, cw_ref, cb_ref, dtb_ref, alog_ref, dskip_ref, nw_ref,
                y_ref, xbuf, state):
    lc = SSM_CHUNK
    rows = z_ref.shape[0]
    first = pl.program_id(1) == 0

    @pl.when(first)
    def _():
        xbuf[0:HALO, :] = jnp.zeros((HALO, SSM_CONV_DIM), F32)
        state[...] = jnp.zeros_like(state)

    @pl.when(jnp.logical_not(first))
    def _():
        xbuf[0:HALO, :] = xbuf[rows:rows + HALO, :]

    xbuf[HALO:, :] = xbc_ref[...].astype(F32)
    a_neg = -jnp.exp(alog_ref[...])
    ri = lax.broadcasted_iota(jnp.int32, (lc, lc), 0)
    ci = lax.broadcasted_iota(jnp.int32, (lc, lc), 1)
    lower = ri >= ci
    tril = jnp.where(lower, 1.0, 0.0)
    lane = lax.broadcasted_iota(jnp.int32, (lc, LANES), 1)
    low_half = lane < SSM_HEAD_DIM

    def expand(col_src):
        parts = []
        for p in range(SSM_HEADS // 2):
            lo = jnp.broadcast_to(col_src[:, 2 * p:2 * p + 1], (lc, LANES))
            hi = jnp.broadcast_to(col_src[:, 2 * p + 1:2 * p + 2], (lc, LANES))
            parts.append(jnp.where(low_half, lo, hi))
        return jnp.concatenate(parts, axis=1)

    for r0 in range(0, rows, lc):
        conv = cb_ref[...] + cw_ref[SSM_CONV - 1:SSM_CONV, :] * xbuf[HALO + r0:HALO + r0 + lc, :]
        for k in range(SSM_CONV - 1):
            back = SSM_CONV - 1 - k
            conv = conv + cw_ref[k:k + 1, :] * xbuf[HALO - back + r0:HALO - back + r0 + lc, :]
        xc = conv * _sigmoid(conv)
        xs = xc[:, :SSM_INNER]
        bm16 = xc[:, SSM_INNER:SSM_INNER + LANES].astype(BF16)
        cm16 = xc[:, SSM_INNER + LANES:].astype(BF16)

        dt = _softplus(dt_ref[r0:r0 + lc, :] + dtb_ref[...])
        a = dt * a_neg
        a_cs = jnp.dot(tril, a, preferred_element_type=F32,
                       precision=lax.Precision.HIGHEST)
        a_cs_t = a_cs.T

        dt_x = expand(dt)
        acs_x = expand(a_cs)
        xdt = xs * dt_x
        last_x = acs_x[lc - 1:lc, :]
        decay_to_end = jnp.exp(last_x - acs_x)
        decay_from_start = jnp.exp(acs_x)
        chunk_decay = jnp.exp(last_x)

        cb = []
        for g in range(SSM_GROUPS):
            cg = jnp.where(low_half if g == 0 else ~low_half, cm16, jnp.zeros_like(cm16))
            cb.append(lax.dot_general(cg, bm16, (((1,), (1,)), ((), ())), preferred_element_type=F32))
        xdt16 = xdt.astype(BF16)
        y_pairs = []
        for p in range(SSM_HEADS // 2):
            xp = xdt16[:, p * LANES:(p + 1) * LANES]
            yp = None
            for half in range(2):
                hd = 2 * p + half
                g = hd // (SSM_HEADS // SSM_GROUPS)
                diff = a_cs[:, hd:hd + 1] - a_cs_t[hd:hd + 1, :]
                dec = jnp.exp(jnp.where(lower, diff, -jnp.inf))
                scores = (cb[g] * dec).astype(BF16)
                xh = jnp.where(low_half if half == 0 else ~low_half, xp, jnp.zeros_like(xp))
                t = jnp.dot(scores, xh, preferred_element_type=F32)
                yp = t if yp is None else yp + t
            y_pairs.append(yp)
        y = jnp.concatenate(y_pairs, axis=1)

        prev = state[...]
        y = y + jnp.dot(cm16, prev.astype(BF16), preferred_element_type=F32) * decay_from_start
        xw = (xdt * decay_to_end).astype(BF16)
        cs = lax.dot_general(bm16, xw, (((0,), (0,)), ((), ())), preferred_element_type=F32)
        srow = lax.broadcasted_iota(jnp.int32, cs.shape, 0) // SSM_STATE
        scol = lax.broadcasted_iota(jnp.int32, cs.shape, 1) // (SSM_INNER // SSM_GROUPS)
        state[...] = prev * chunk_decay + jnp.where(srow == scol, cs, 0.0)

        y = y + xs * dskip_ref[...]
        zf = z_ref[r0:r0 + lc, :].astype(F32)
        y = y * (zf * _sigmoid(zf))
        gw = SSM_INNER // SSM_GROUPS
        outs = []
        for g in range(SSM_GROUPS):
            yg = y[:, g * gw:(g + 1) * gw]
            ms = jnp.mean(yg * yg, axis=-1, keepdims=True)
            outs.append(yg * lax.rsqrt(ms + NORM_EPS))
        y_ref[r0:r0 + lc, :] = (jnp.concatenate(outs, axis=1) * nw_ref[...]).astype(BF16)


def _ssd(z, xbc, dt, conv_w, conv_b, dt_bias, a_log, d_skip, norm_w, bsz, seq_len):
    n = z.shape[0]
    rows = min(SSD_CHUNKS_PER_STEP * SSM_CHUNK, seq_len)
    steps = seq_len // rows
    row = lambda b, c: (b * steps + c, 0)
    return pl.pallas_call(
        _ssd_kernel,
        grid=(bsz, steps),
        in_specs=[
            pl.BlockSpec((rows, SSM_INNER), row),
            pl.BlockSpec((rows, SSM_CONV_DIM), row),
            pl.BlockSpec((rows, LANES), row),
            _resident((SSM_CONV, SSM_CONV_DIM)),
            _resident((1, SSM_CONV_DIM)),
            _resident((1, LANES)),
            _resident((1, LANES)),
            _resident((1, SSM_INNER)),
            _resident((1, SSM_INNER)),
        ],
        out_specs=pl.BlockSpec((rows, SSM_INNER), row),
        out_shape=jax.ShapeDtypeStruct((n, SSM_INNER), BF16),
        scratch_shapes=[pltpu.VMEM((rows + HALO, SSM_CONV_DIM), F32),
                        pltpu.VMEM((SSM_GROUPS * SSM_STATE, SSM_INNER), F32)],
        compiler_params=_compiler_params(2),
        name="ssd_mixer",
    )(z, xbc, dt, conv_w, conv_b, dt_bias, a_log, d_skip, norm_w)


def _merge_kernel(x_ref, mod_ref, gpre_ref, gpost_ref, fa_ref, fb_ref, fc_ref,
                  wg_ref, wa_ref, wb_ref, wc_ref, wo_ref, o_ref):
    d = D_MODEL
    shift = mod_ref[0, 0:1, :]
    scale = mod_ref[0, 1:2, :]
    gate = mod_ref[0, 2:3, :]
    for rs in _sub_tiles(x_ref.shape[0]):
        x = x_ref[rs, :]
        h = _modulated_prenorm(x, gpre_ref[...], scale, shift).astype(BF16)
        merged = None
        for br, (f_ref, w_ref) in enumerate(((fa_ref, wa_ref), (fb_ref, wb_ref), (fc_ref, wc_ref))):
            gl = jnp.dot(h, wg_ref[:, br * d:(br + 1) * d], preferred_element_type=F32)
            yb = jnp.dot(f_ref[rs, :], w_ref[...], preferred_element_type=F32)
            t = _sigmoid(gl) * yb
            merged = t if merged is None else merged + t
        mix = jnp.dot(merged.astype(BF16), wo_ref[...], preferred_element_type=F32)
        ms = jnp.mean(mix * mix, axis=-1, keepdims=True)
        o_ref[rs, :] = x + gate * (mix * lax.rsqrt(ms + NORM_EPS) * gpost_ref[...])


def _merge(xf, mod_l, g_pre, g_post, fa, fb, fc, wg, wa, wb, wc, wo, seq_len):
    n, d = xf.shape
    tm = min(TOKEN_TILE, seq_len)
    tiles_per_seq = seq_len // tm
    row = lambda i: (i, 0)
    return pl.pallas_call(
        _merge_kernel,
        grid=(n // tm,),
        in_specs=[
            pl.BlockSpec((tm, d), row),
            pl.BlockSpec((1, N_MOD, d), lambda i: (i // tiles_per_seq, 0, 0)),
            _resident((1, d)),
            _resident((1, d)),
            pl.BlockSpec((tm, SC_WIDTH), row),
            pl.BlockSpec((tm, SB_WIDTH), row),
            pl.BlockSpec((tm, SSM_INNER), row),
            _resident((d, 3 * d)),
            _resident((SC_WIDTH, d)),
            _resident((SB_WIDTH, d)),
            _resident((SSM_INNER, d)),
            _resident((d, d)),
        ],
        out_specs=pl.BlockSpec((tm, d), row),
        out_shape=jax.ShapeDtypeStruct((n, d), F32),
        compiler_params=_compiler_params(1),
        name="merge_outproj",
    )(xf, mod_l, g_pre, g_post, fa, fb, fc, wg, wa, wb, wc, wo)


def _ffn_kernel(x_ref, mod_ref, gpre_ref, gpost_ref, win_ref, wout_ref, o_ref):
    shift = mod_ref[0, 3:4, :]
    scale = mod_ref[0, 4:5, :]
    gate = mod_ref[0, 5:6, :]
    for rs in _sub_tiles(x_ref.shape[0]):
        x = x_ref[rs, :]
        h = _modulated_prenorm(x, gpre_ref[...], scale, shift).astype(BF16)
        gt = jnp.dot(h, win_ref[:, :FFN_HIDDEN], preferred_element_type=F32)
        up = jnp.dot(h, win_ref[:, FFN_HIDDEN:], preferred_element_type=F32)
        act = (gt * _sigmoid(gt) * up).astype(BF16)
        f = jnp.dot(act, wout_ref[...], preferred_element_type=F32)
        ms = jnp.mean(f * f, axis=-1, keepdims=True)
        o_ref[rs, :] = x + gate * (f * lax.rsqrt(ms + NORM_EPS) * gpost_ref[...])


def _ffn(xf, mod_l, g_pre, g_post, w_in, w_out, seq_len):
    n, d = xf.shape
    tm = min(TOKEN_TILE, seq_len)
    tiles_per_seq = seq_len // tm
    row = lambda i: (i, 0)
    return pl.pallas_call(
        _ffn_kernel,
        grid=(n // tm,),
        in_specs=[
            pl.BlockSpec((tm, d), row),
            pl.BlockSpec((1, N_MOD, d), lambda i: (i // tiles_per_seq, 0, 0)),
            _resident((1, d)),
            _resident((1, d)),
            _resident((d, 2 * FFN_HIDDEN)),
            _resident((FFN_HIDDEN, d)),
        ],
        out_specs=pl.BlockSpec((tm, d), row),
        out_shape=jax.ShapeDtypeStruct((n, d), F32),
        compiler_params=_compiler_params(1),
        name="swiglu_ffn",
    )(xf, mod_l, g_pre, g_post, w_in, w_out)


def _pad_lanes(v):
    return jnp.pad(v.astype(F32), (0, LANES - v.shape[0])).reshape(1, LANES)


def kernel(x, c, mod_w, mod_b, g_pre_mix, g_post_mix, g_pre_ffn, g_post_ffn, w_in, sc_conv_w,
           ssm_conv_w, ssm_conv_b, ssm_dt_bias, ssm_a_log, ssm_d, ssm_norm_w, w_sc_out, w_sb_out,
           w_ssm_out, w_o, w_ffn_in, w_ffn_out):
    bsz, seq_len, d = x.shape
    depth = mod_w.shape[0]
    assert d == D_MODEL and seq_len % (ATTN_Q_BLOCKS * SB_BLOCK) == 0 and seq_len % SSM_CHUNK == 0
    mod = _modulation(c, mod_w, mod_b).reshape(depth, bsz, N_MOD, d)
    xf = x.reshape(bsz * seq_len, d)
    q_scale = SB_HEAD_DIM ** -0.5 * LOG2_E
    for l in range(depth):
        wl = w_in[l]
        w1 = jnp.concatenate([
            wl[:, :_OFF_QKV],
            wl[:, _OFF_QKV:_OFF_QKV + SB_WIDTH] * q_scale,
            wl[:, _OFF_QKV + SB_WIDTH:_OFF_DT],
            jnp.pad(wl[:, _OFF_DT:_OFF_DT + SSM_HEADS], ((0, 0), (0, LANES - SSM_HEADS))),
        ], axis=1).astype(BF16)
        wg = wl[:, _OFF_DT + SSM_HEADS:].astype(BF16)
        row = lambda v: v.reshape(1, -1).astype(F32)

        fa, qkv, z, xbc, dt = _inproj(xf, mod[l], row(g_pre_mix[l]), w1, sc_conv_w[l].astype(F32), seq_len)
        fb = _attention(qkv, bsz, seq_len)
        fc = _ssd(z, xbc, dt, ssm_conv_w[l].astype(F32), row(ssm_conv_b[l]), _pad_lanes(ssm_dt_bias[l]),
                  _pad_lanes(ssm_a_log[l]),
                  row(jnp.repeat(ssm_d[l], SSM_HEAD_DIM)), row(ssm_norm_w[l]), bsz, seq_len)
        xf = _merge(xf, mod[l], row(g_pre_mix[l]), row(g_post_mix[l]), fa, fb, fc, wg,
                    w_sc_out[l].astype(BF16), w_sb_out[l].astype(BF16), w_ssm_out[l].astype(BF16),
                    w_o[l].astype(BF16), seq_len)
        xf = _ffn(xf, mod[l], row(g_pre_ffn[l]), row(g_post_ffn[l]), w_ffn_in[l].astype(BF16),
                  w_ffn_out[l].astype(BF16), seq_len)
    return xf.reshape(bsz, seq_len, d)
```

```python
import functools

import jax
import jax.numpy as jnp
from jax import lax
from jax.experimental import pallas as pl
from jax.experimental.pallas import tpu as pltpu

D_MODEL = 1024
SC_WIDTH = 256
SC_KERNEL = 3
SB_HEAD_DIM = 64
SB_WIDTH = 256
SB_BLOCK = 128
SSM_INNER = 512
SSM_HEAD_DIM = 64
SSM_HEADS = 8
SSM_GROUPS = 2
SSM_STATE = 64
SSM_CONV = 4
SSM_CHUNK = 256
SSM_CONV_DIM = SSM_INNER + 2 * SSM_GROUPS * SSM_STATE
FFN_HIDDEN = 2816
NORM_EPS = 1e-6
LOG2_E = 1.4426950408889634
F32_SIGN_BIT = 0x80000000
N_MOD = 6

LANES = 128
SUBLANES = 8
VMEM_LIMIT_BYTES = 56 * 1024 * 1024

TOKEN_TILE = 1024
SUB_TILE = 512
ATTN_Q_BLOCKS = 8
ATTN_UNROLL = 8
SSD_CHUNKS_PER_STEP = 4
HALO = SUBLANES

F32 = jnp.float32
BF16 = jnp.bfloat16


def _compiler_params(n_axes):
    return pltpu.CompilerParams(
        dimension_semantics=("arbitrary",) * n_axes,
        vmem_limit_bytes=VMEM_LIMIT_BYTES,
    )


def _resident(block_shape):
    zeros = (0,) * len(block_shape)
    return pl.BlockSpec(block_shape, lambda *_: zeros, pipeline_mode=pl.Buffered(1))


def _sub_tiles(rows):
    sub = min(SUB_TILE, rows)
    assert rows % sub == 0
    return [slice(r, r + sub) for r in range(0, rows, sub)]


def _sigmoid(v):
    return 1.0 / (1.0 + jnp.exp(-v))


def _softplus(v):
    return jnp.maximum(v, 0.0) + jnp.log(1.0 + jnp.exp(-jnp.abs(v)))


def _modulated_prenorm(x, g, scale, shift):
    ms = jnp.mean(x * x, axis=-1, keepdims=True)
    return x * lax.rsqrt(ms + NORM_EPS) * (g * (1.0 + scale)) + shift


def _mod_kernel(c_ref, w_ref, b_ref, o_ref):
    c = c_ref[...]
    sc = c * _sigmoid(c)
    o_ref[0] = jnp.dot(sc, w_ref[0], preferred_element_type=F32,
                       precision=lax.Precision.HIGHEST) + b_ref[0]


def _modulation(c, mod_w, mod_b):
    depth, d, n_out = mod_w.shape
    bsz = c.shape[0]
    tn = D_MODEL
    return pl.pallas_call(
        _mod_kernel,
        grid=(depth, n_out // tn),
        in_specs=[
            pl.BlockSpec((bsz, d), lambda l, j: (0, 0)),
            pl.BlockSpec((1, d, tn), lambda l, j: (l, 0, j)),
            pl.BlockSpec((1, 1, tn), lambda l, j: (l, 0, j)),
        ],
        out_specs=pl.BlockSpec((1, bsz, tn), lambda l, j: (l, 0, j)),
        out_shape=jax.ShapeDtypeStruct((depth, bsz, n_out), F32),
        compiler_params=_compiler_params(2),
        name="adaln_modulation",
    )(c, mod_w, mod_b.reshape(depth, 1, n_out))


_A_COLS = 3 * SC_WIDTH
_QKV_COLS = 3 * SB_WIDTH
_OFF_QKV = _A_COLS
_OFF_Z = _OFF_QKV + _QKV_COLS
_OFF_XBC = _OFF_Z + SSM_INNER
_OFF_DT = _OFF_XBC + SSM_CONV_DIM
_W1_COLS = _OFF_DT + LANES


def _inproj_kernel(tiles_per_seq, x_ref, mod_ref, g_ref, w_ref, cw_ref,
                   fa_ref, qkv_ref, z_ref, xbc_ref, dt_ref, ubuf):
    i = pl.program_id(0)
    tm = x_ref.shape[0]
    shift = mod_ref[0, 0:1, :]
    scale = mod_ref[0, 1:2, :]

    @pl.when(i % tiles_per_seq == 0)
    def _():
        ubuf[0:HALO, :] = jnp.zeros((HALO, SC_WIDTH), F32)

    @pl.when(i % tiles_per_seq != 0)
    def _():
        ubuf[0:HALO, :] = ubuf[tm:tm + HALO, :]

    for rs in _sub_tiles(tm):
        h = _modulated_prenorm(x_ref[rs, :], g_ref[...], scale, shift).astype(BF16)

        def proj(lo, hi):
            return jnp.dot(h, w_ref[:, lo:hi], preferred_element_type=F32)

        qkv_ref[rs, :] = proj(_OFF_QKV, _OFF_Z).astype(BF16)
        z_ref[rs, :] = proj(_OFF_Z, _OFF_XBC).astype(BF16)
        xbc_ref[rs, :] = proj(_OFF_XBC, _OFF_DT).astype(BF16)
        dt_ref[rs, :] = proj(_OFF_DT, _W1_COLS)

        pa = proj(0, _A_COLS)
        u = pa[:, SC_WIDTH:2 * SC_WIDTH] * pa[:, 2 * SC_WIDTH:]
        ubuf[HALO + rs.start:HALO + rs.stop, :] = u
        conv = cw_ref[SC_KERNEL - 1:SC_KERNEL, :] * u
        for k in range(SC_KERNEL - 1):
            back = SC_KERNEL - 1 - k
            conv = conv + cw_ref[k:k + 1, :] * ubuf[HALO - back + rs.start:HALO - back + rs.stop, :]
        fa_ref[rs, :] = (pa[:, :SC_WIDTH] * conv).astype(BF16)


def _inproj(xf, mod_l, g_pre, w1, sc_conv_w, seq_len):
    n, d = xf.shape
    tm = min(TOKEN_TILE, seq_len)
    tiles_per_seq = seq_len // tm
    row = lambda i: (i, 0)
    return pl.pallas_call(
        functools.partial(_inproj_kernel, tiles_per_seq),
        grid=(n // tm,),
        in_specs=[
            pl.BlockSpec((tm, d), row),
            pl.BlockSpec((1, N_MOD, d), lambda i: (i // tiles_per_seq, 0, 0)),
            _resident((1, d)),
            _resident((d, _W1_COLS)),
            _resident((SC_KERNEL, SC_WIDTH)),
        ],
        out_specs=[
            pl.BlockSpec((tm, SC_WIDTH), row),
            pl.BlockSpec((tm, _QKV_COLS), row),
            pl.BlockSpec((tm, SSM_INNER), row),
            pl.BlockSpec((tm, SSM_CONV_DIM), row),
            pl.BlockSpec((tm, LANES), row),
        ],
        out_shape=[
            jax.ShapeDtypeStruct((n, SC_WIDTH), BF16),
            jax.ShapeDtypeStruct((n, _QKV_COLS), BF16),
            jax.ShapeDtypeStruct((n, SSM_INNER), BF16),
            jax.ShapeDtypeStruct((n, SSM_CONV_DIM), BF16),
            jax.ShapeDtypeStruct((n, LANES), F32),
        ],
        scratch_shapes=[pltpu.VMEM((tm + HALO, SC_WIDTH), F32)],
        compiler_params=_compiler_params(1),
        name="inproj_shortconv",
    )(xf, mod_l, g_pre, w1, sc_conv_w)


def _attn_kernel(q_ref, k_ref, v_ref, o_ref, acc, carry, lb_buf, sp_buf, ks_buf, vs_buf):
    blk = SB_BLOCK
    first_blk = pl.program_id(2) * ATTN_Q_BLOCKS
    q = q_ref[...]

    lane_k = lax.broadcasted_iota(jnp.int32, (blk, LANES), 1)
    low_half = lane_k < SB_HEAD_DIM
    jj = lax.broadcasted_iota(jnp.int32, (2 * blk, 2 * blk), 0)
    ss = lax.broadcasted_iota(jnp.int32, (2 * blk, 2 * blk), 1)
    same_head = (jj // blk) == (ss // blk)
    later_mat = jnp.where(same_head & (jj > ss), 1.0, 0.0).astype(BF16)
    total_mat = jnp.where(same_head, 1.0, 0.0).astype(BF16)
    diag_row = lax.broadcasted_iota(jnp.int32, (blk, 2 * blk), 0)
    diag_key = lax.broadcasted_iota(jnp.int32, (blk, 2 * blk), 1) % blk
    causal = diag_key < diag_row

    @pl.when(pl.program_id(2) == 0)
    def _():
        def fill(j, _):
            src = pl.ds(pl.multiple_of(j * blk, blk), blk)
            for ref, dst in ((k_ref, ks_buf), (v_ref, vs_buf)):
                t = ref[src, :]
                zero = jnp.zeros_like(t)
                dst[pl.ds(pl.multiple_of(2 * j * blk, 2 * blk), blk), :] = jnp.where(low_half, t, zero)
                dst[pl.ds(pl.multiple_of(2 * j * blk + blk, blk), blk), :] = jnp.where(low_half, zero, t)
            return 0

        lax.fori_loop(0, k_ref.shape[0] // blk, fill, 0)

    def stacked(buf, j):
        return buf[pl.ds(pl.multiple_of(2 * j * blk, 2 * blk), 2 * blk), :]

    def mask_diagonal(t):
        top = jnp.where(causal, t[:blk], 0.0)
        return top if t.shape[0] == blk else jnp.concatenate([top, t[blk:]], axis=0)

    def softplus_terms(j, r0):
        z = lax.dot_general(q[r0:], stacked(ks_buf, j), (((1,), (1,)), ((), ())), preferred_element_type=F32)
        neg_abs = pltpu.bitcast(pltpu.bitcast(z, jnp.uint32) | jnp.uint32(F32_SIGN_BIT), F32)
        sp = jnp.maximum(z, 0.0) + jnp.log2(1.0 + jnp.exp2(neg_abs))
        return z - sp, sp

    def attend(j, r0, log_beta, sp16, diagonal):
        wp = jnp.dot(sp16, later_mat, preferred_element_type=F32)
        tp = jnp.dot(sp16, total_mat, preferred_element_type=F32)
        c = carry[r0:, :]
        att = jnp.exp2(log_beta - (wp + c))
        if diagonal:
            att = mask_diagonal(att)
        carry[r0:, :] = c + tp
        acc[r0:, :] += jnp.dot(att.astype(BF16), stacked(vs_buf, j), preferred_element_type=F32)

    def step(j, r0, diagonal):
        log_beta, sp = softplus_terms(j, r0)
        if diagonal:
            sp = mask_diagonal(sp)
        attend(j, r0, log_beta, sp.astype(BF16), diagonal)

    def stage(j):
        log_beta, sp = softplus_terms(j, 0)
        lb_buf[...] = log_beta
        sp_buf[...] = sp.astype(BF16)

    carry[...] = jnp.zeros_like(carry)
    acc[...] = jnp.zeros_like(acc)
    for kb in range(ATTN_Q_BLOCKS - 1, -1, -1):
        step(first_blk + kb, kb * blk, True)

    stage(jnp.maximum(first_blk - 1, 0))

    def body(t, _):
        j = first_blk - 1 - t * ATTN_UNROLL
        attend(j, 0, lb_buf[...], sp_buf[...], False)
        for u in range(1, ATTN_UNROLL):
            step(j - u, 0, False)
        stage(jnp.maximum(j - ATTN_UNROLL, 0))
        return 0

    lax.fori_loop(0, first_blk // ATTN_UNROLL, body, 0)
    o_ref[...] = acc[...].astype(BF16)


def _attention(qkv, bsz, seq_len):
    n = qkv.shape[0]
    rows = ATTN_Q_BLOCKS * SB_BLOCK
    steps = seq_len // rows
    pairs = SB_WIDTH // LANES
    return pl.pallas_call(
        _attn_kernel,
        grid=(bsz, pairs, steps),
        in_specs=[
            pl.BlockSpec((rows, LANES), lambda b, p, i: (b * steps + i, p)),
            pl.BlockSpec((seq_len, LANES), lambda b, p, i: (b, pairs + p)),
            pl.BlockSpec((seq_len, LANES), lambda b, p, i: (b, 2 * pairs + p)),
        ],
        out_specs=pl.BlockSpec((rows, LANES), lambda b, p, i: (b * steps + i, p)),
        out_shape=jax.ShapeDtypeStruct((n, SB_WIDTH), BF16),
        scratch_shapes=[pltpu.VMEM((rows, LANES), F32), pltpu.VMEM((rows, 2 * SB_BLOCK), F32),
                        pltpu.VMEM((rows, 2 * SB_BLOCK), F32), pltpu.VMEM((rows, 2 * SB_BLOCK), BF16),
                        pltpu.VMEM((2 * seq_len, LANES), BF16), pltpu.VMEM((2 * seq_len, LANES), BF16)],
        compiler_params=_compiler_params(3),
        name="stickbreak_attention",
    )(qkv, qkv, qkv)


def _ssd_kernel(z_ref, xbc_ref, dt_ref, cw_ref, cb_ref, dtb_ref, alog_ref, dskip_ref, nw_ref,
                y_ref, xbuf, state):
    lc = SSM_CHUNK
    rows = z_ref.shape[0]
    first = pl.program_id(1) == 0

    @pl.when(first)
    def _():
        xbuf[0:HALO, :] = jnp.zeros((HALO, SSM_CONV_DIM), F32)
        state[...] = jnp.zeros_like(state)

    @pl.when(jnp.logical_not(first))
    def _():
        xbuf[0:HALO, :] = xbuf[rows:rows + HALO, :]

    xbuf[HALO:, :] = xbc_ref[...].astype(F32)
    a_neg = -jnp.exp(alog_ref[...])
    ri = lax.broadcasted_iota(jnp.int32, (lc, lc), 0)
    ci = lax.broadcasted_iota(jnp.int32, (lc, lc), 1)
    lower = ri >= ci
    tril = jnp.where(lower, 1.0, 0.0)
    lane = lax.broadcasted_iota(jnp.int32, (lc, LANES), 1)
    low_half = lane < SSM_HEAD_DIM

    def expand(col_src):
        parts = []
        for p in range(SSM_HEADS // 2):
            lo = jnp.broadcast_to(col_src[:, 2 * p:2 * p + 1], (lc, LANES))
            hi = jnp.broadcast_to(col_src[:, 2 * p + 1:2 * p + 2], (lc, LANES))
            parts.append(jnp.where(low_half, lo, hi))
        return jnp.concatenate(parts, axis=1)

    for r0 in range(0, rows, lc):
        conv = cb_ref[...] + cw_ref[SSM_CONV - 1:SSM_CONV, :] * xbuf[HALO + r0:HALO + r0 + lc, :]
        for k in range(SSM_CONV - 1):
            back = SSM_CONV - 1 - k
            conv = conv + cw_ref[k:k + 1, :] * xbuf[HALO - back + r0:HALO - back + r0 + lc, :]
        xc = conv * _sigmoid(conv)
        xs = xc[:, :SSM_INNER]
        bm16 = xc[:, SSM_INNER:SSM_INNER + LANES].astype(BF16)
        cm16 = xc[:, SSM_INNER + LANES:].astype(BF16)

        dt = _softplus(dt_ref[r0:r0 + lc, :] + dtb_ref[...])
        a = dt * a_neg
        a_cs = jnp.dot(tril, a, preferred_element_type=F32,
                       precision=lax.Precision.HIGHEST)
        a_cs_t = a_cs.T

        dt_x = expand(dt)
        acs_x = expand(a_cs)
        xdt = xs * dt_x
        last_x = acs_x[lc - 1:lc, :]
        decay_to_end = jnp.exp(last_x - acs_x)
        decay_from_start = jnp.exp(acs_x)
        chunk_decay = jnp.exp(last_x)

        cb = []
        for g in range(SSM_GROUPS):
            cg = jnp.where(low_half if g == 0 else ~low_half, cm16, jnp.zeros_like(cm16))
            cb.append(lax.dot_general(cg, bm16, (((1,), (1,)), ((), ())), preferred_element_type=F32))
        xdt16 = xdt.astype(BF16)
        y_pairs = []
        for p in range(SSM_HEADS // 2):
            xp = xdt16[:, p * LANES:(p + 1) * LANES]
            yp = None
            for half in range(2):
                hd = 2 * p + half
                g = hd // (SSM_HEADS // SSM_GROUPS)
                diff = a_cs[:, hd:hd + 1] - a_cs_t[hd:hd + 1, :]
                dec = jnp.exp(jnp.where(lower, diff, -jnp.inf))
                scores = (cb[g] * dec).astype(BF16)
                xh = jnp.where(low_half if half == 0 else ~low_half, xp, jnp.zeros_like(xp))
                t = jnp.dot(scores, xh, preferred_element_type=F32)
                yp = t if yp is None else yp + t
            y_pairs.append(yp)
        y = jnp.concatenate(y_pairs, axis=1)

        prev = state[...]
        y = y + jnp.dot(cm16, prev.astype(BF16), preferred_element_type=F32) * decay_from_start
        xw = (xdt * decay_to_end).astype(BF16)
        cs = lax.dot_general(bm16, xw, (((0,), (0,)), ((), ())), preferred_element_type=F32)
        srow = lax.broadcasted_iota(jnp.int32, cs.shape, 0) // SSM_STATE
        scol = lax.broadcasted_iota(jnp.int32, cs.shape, 1) // (SSM_INNER // SSM_GROUPS)
        state[...] = prev * chunk_decay + jnp.where(srow == scol, cs, 0.0)

        y = y + xs * dskip_ref[...]
        zf = z_ref[r0:r0 + lc, :].astype(F32)
        y = y * (zf * _sigmoid(zf))
        gw = SSM_INNER // SSM_GROUPS
        outs = []
        for g in range(SSM_GROUPS):
            yg = y[:, g * gw:(g + 1) * gw]
            ms = jnp.mean(yg * yg, axis=-1, keepdims=True)
            outs.append(yg * lax.rsqrt(ms + NORM_EPS))
        y_ref[r0:r0 + lc, :] = (jnp.concatenate(outs, axis=1) * nw_ref[...]).astype(BF16)


def _ssd(z, xbc, dt, conv_w, conv_b, dt_bias, a_log, d_skip, norm_w, bsz, seq_len):
    n = z.shape[0]
    rows = min(SSD_CHUNKS_PER_STEP * SSM_CHUNK, seq_len)
    steps = seq_len // rows
    row = lambda b, c: (b * steps + c, 0)
    return pl.pallas_call(
        _ssd_kernel,
        grid=(bsz, steps),
        in_specs=[
            pl.BlockSpec((rows, SSM_INNER), row),
            pl.BlockSpec((rows, SSM_CONV_DIM), row),
            pl.BlockSpec((rows, LANES), row),
            _resident((SSM_CONV, SSM_CONV_DIM)),
            _resident((1, SSM_CONV_DIM)),
            _resident((1, LANES)),
            _resident((1, LANES)),
            _resident((1, SSM_INNER)),
            _resident((1, SSM_INNER)),
        ],
        out_specs=pl.BlockSpec((rows, SSM_INNER), row),
        out_shape=jax.ShapeDtypeStruct((n, SSM_INNER), BF16),
        scratch_shapes=[pltpu.VMEM((rows + HALO, SSM_CONV_DIM), F32),
                        pltpu.VMEM((SSM_GROUPS * SSM_STATE, SSM_INNER), F32)],
        compiler_params=_compiler_params(2),
        name="ssd_mixer",
    )(z, xbc, dt, conv_w, conv_b, dt_bias, a_log, d_skip, norm_w)


def _merge_kernel(x_ref, mod_ref, gpre_ref, gpost_ref, fa_ref, fb_ref, fc_ref,
                  wg_ref, wa_ref, wb_ref, wc_ref, wo_ref, o_ref):
    d = D_MODEL
    shift = mod_ref[0, 0:1, :]
    scale = mod_ref[0, 1:2, :]
    gate = mod_ref[0, 2:3, :]
    for rs in _sub_tiles(x_ref.shape[0]):
        x = x_ref[rs, :]
        h = _modulated_prenorm(x, gpre_ref[...], scale, shift).astype(BF16)
        merged = None
        for br, (f_ref, w_ref) in enumerate(((fa_ref, wa_ref), (fb_ref, wb_ref), (fc_ref, wc_ref))):
            gl = jnp.dot(h, wg_ref[:, br * d:(br + 1) * d], preferred_element_type=F32)
            yb = jnp.dot(f_ref[rs, :], w_ref[...], preferred_element_type=F32)
            t = _sigmoid(gl) * yb
            merged = t if merged is None else merged + t
        mix = jnp.dot(merged.astype(BF16), wo_ref[...], preferred_element_type=F32)
        ms = jnp.mean(mix * mix, axis=-1, keepdims=True)
        o_ref[rs, :] = x + gate * (mix * lax.rsqrt(ms + NORM_EPS) * gpost_ref[...])


def _merge(xf, mod_l, g_pre, g_post, fa, fb, fc, wg, wa, wb, wc, wo, seq_len):
    n, d = xf.shape
    tm = min(TOKEN_TILE, seq_len)
    tiles_per_seq = seq_len // tm
    row = lambda i: (i, 0)
    return pl.pallas_call(
        _merge_kernel,
        grid=(n // tm,),
        in_specs=[
            pl.BlockSpec((tm, d), row),
            pl.BlockSpec((1, N_MOD, d), lambda i: (i // tiles_per_seq, 0, 0)),
            _resident((1, d)),
            _resident((1, d)),
            pl.BlockSpec((tm, SC_WIDTH), row),
            pl.BlockSpec((tm, SB_WIDTH), row),
            pl.BlockSpec((tm, SSM_INNER), row),
            _resident((d, 3 * d)),
            _resident((SC_WIDTH, d)),
            _resident((SB_WIDTH, d)),
            _resident((SSM_INNER, d)),
            _resident((d, d)),
        ],
        out_specs=pl.BlockSpec((tm, d), row),
        out_shape=jax.ShapeDtypeStruct((n, d), F32),
        compiler_params=_compiler_params(1),
        name="merge_outproj",
    )(xf, mod_l, g_pre, g_post, fa, fb, fc, wg, wa, wb, wc, wo)


def _ffn_kernel(x_ref, mod_ref, gpre_ref, gpost_ref, win_ref, wout_ref, o_ref):
    shift = mod_ref[0, 3:4, :]
    scale = mod_ref[0, 4:5, :]
    gate = mod_ref[0, 5:6, :]
    for rs in _sub_tiles(x_ref.shape[0]):
        x = x_ref[rs, :]
        h = _modulated_prenorm(x, gpre_ref[...], scale, shift).astype(BF16)
        gt = jnp.dot(h, win_ref[:, :FFN_HIDDEN], preferred_element_type=F32)
        up = jnp.dot(h, win_ref[:, FFN_HIDDEN:], preferred_element_type=F32)
        act = (gt * _sigmoid(gt) * up).astype(BF16)
        f = jnp.dot(act, wout_ref[...], preferred_element_type=F32)
        ms = jnp.mean(f * f, axis=-1, keepdims=True)
        o_ref[rs, :] = x + gate * (f * lax.rsqrt(ms + NORM_EPS) * gpost_ref[...])


def _ffn(xf, mod_l, g_pre, g_post, w_in, w_out, seq_len):
    n, d = xf.shape
    tm = min(TOKEN_TILE, seq_len)
    tiles_per_seq = seq_len // tm
    row = lambda i: (i, 0)
    return pl.pallas_call(
        _ffn_kernel,
        grid=(n // tm,),
        in_specs=[
            pl.BlockSpec((tm, d), row),
            pl.BlockSpec((1, N_MOD, d), lambda i: (i // tiles_per_seq, 0, 0)),
            _resident((1, d)),
            _resident((1, d)),
            _resident((d, 2 * FFN_HIDDEN)),
            _resident((FFN_HIDDEN, d)),
        ],
        out_specs=pl.BlockSpec((tm, d), row),
        out_shape=jax.ShapeDtypeStruct((n, d), F32),
        compiler_params=_compiler_params(1),
        name="swiglu_ffn",
    )(xf, mod_l, g_pre, g_post, w_in, w_out)


def _pad_lanes(v):
    return jnp.pad(v.astype(F32), (0, LANES - v.shape[0])).reshape(1, LANES)


def kernel(x, c, mod_w, mod_b, g_pre_mix, g_post_mix, g_pre_ffn, g_post_ffn, w_in, sc_conv_w,
           ssm_conv_w, ssm_conv_b, ssm_dt_bias, ssm_a_log, ssm_d, ssm_norm_w, w_sc_out, w_sb_out,
           w_ssm_out, w_o, w_ffn_in, w_ffn_out):
    bsz, seq_len, d = x.shape
    depth = mod_w.shape[0]
    assert d == D_MODEL and seq_len % (ATTN_Q_BLOCKS * SB_BLOCK) == 0 and seq_len % SSM_CHUNK == 0
    mod = _modulation(c, mod_w, mod_b).reshape(depth, bsz, N_MOD, d)
    xf = x.reshape(bsz * seq_len, d)
    q_scale = SB_HEAD_DIM ** -0.5 * LOG2_E
    for l in range(depth):
        wl = w_in[l]
        w1 = jnp.concatenate([
            wl[:, :_OFF_QKV],
            wl[:, _OFF_QKV:_OFF_QKV + SB_WIDTH] * q_scale,
            wl[:, _OFF_QKV + SB_WIDTH:_OFF_DT],
            jnp.pad(wl[:, _OFF_DT:_OFF_DT + SSM_HEADS], ((0, 0), (0, LANES - SSM_HEADS))),
        ], axis=1).astype(BF16)
        wg = wl[:, _OFF_DT + SSM_HEADS:].astype(BF16)
        row = lambda v: v.reshape(1, -1).astype(F32)

        fa, qkv, z, xbc, dt = _inproj(xf, mod[l], row(g_pre_mix[l]), w1, sc_conv_w[l].astype(F32), seq_len)
        fb = _attention(qkv, bsz, seq_len)
        fc = _ssd(z, xbc, dt, ssm_conv_w[l].astype(F32), row(ssm_conv_b[l]), _pad_lanes(ssm_dt_bias[l]),
                  _pad_lanes(ssm_a_log[l]),
                  row(jnp.repeat(ssm_d[l], SSM_HEAD_DIM)), row(ssm_norm_w[l]), bsz, seq_len)
        xf = _merge(xf, mod[l], row(g_pre_mix[l]), row(g_post_mix[l]), fa, fb, fc, wg,
                    w_sc_out[l].astype(BF16), w_sb_out[l].astype(BF16), w_ssm_out[l].astype(BF16),
                    w_o[l].astype(BF16), seq_len)
        xf = _ffn(xf, mod[l], row(g_pre_ffn[l]), row(g_post_ffn[l]), w_ffn_in[l].astype(BF16),
                  w_ffn_out[l].astype(BF16), seq_len)
    return xf.reshape(bsz, seq_len, d)
```

```python
import functools

import jax
import jax.numpy as jnp
from jax import lax
from jax.experimental import pallas as pl
from jax.experimental.pallas import tpu as pltpu

D_MODEL = 1024
SC_WIDTH = 256
SC_KERNEL = 3
SB_HEAD_DIM = 64
SB_WIDTH = 256
SB_BLOCK = 128
SSM_INNER = 512
SSM_HEAD_DIM = 64
SSM_HEADS = 8
SSM_GROUPS = 2
SSM_STATE = 64
SSM_CONV = 4
SSM_CHUNK = 256
SSM_CONV_DIM = SSM_INNER + 2 * SSM_GROUPS * SSM_STATE
FFN_HIDDEN = 2816
NORM_EPS = 1e-6
LOG2_E = 1.4426950408889634
N_MOD = 6

LANES = 128
SUBLANES = 8
VMEM_LIMIT_BYTES = 56 * 1024 * 1024

TOKEN_TILE = 1024
SUB_TILE = 512
ATTN_Q_BLOCKS = 8
ATTN_UNROLL = 8
HALO = SUBLANES

F32 = jnp.float32
BF16 = jnp.bfloat16


def _compiler_params(n_axes):
    return pltpu.CompilerParams(
        dimension_semantics=("arbitrary",) * n_axes,
        vmem_limit_bytes=VMEM_LIMIT_BYTES,
    )


def _resident(block_shape):
    zeros = (0,) * len(block_shape)
    return pl.BlockSpec(block_shape, lambda *_: zeros, pipeline_mode=pl.Buffered(1))


def _sub_tiles(rows):
    sub = min(SUB_TILE, rows)
    assert rows % sub == 0
    return [slice(r, r + sub) for r in range(0, rows, sub)]


def _sigmoid(v):
    return 1.0 / (1.0 + jnp.exp(-v))


def _softplus(v):
    return jnp.maximum(v, 0.0) + jnp.log(1.0 + jnp.exp(-jnp.abs(v)))


def _modulated_prenorm(x, g, scale, shift):
    ms = jnp.mean(x * x, axis=-1, keepdims=True)
    return x * lax.rsqrt(ms + NORM_EPS) * (g * (1.0 + scale)) + shift


def _mod_kernel(c_ref, w_ref, b_ref, o_ref):
    c = c_ref[...]
    sc = c * _sigmoid(c)
    o_ref[0] = jnp.dot(sc, w_ref[0], preferred_element_type=F32,
                       precision=lax.Precision.HIGHEST) + b_ref[0]


def _modulation(c, mod_w, mod_b):
    depth, d, n_out = mod_w.shape
    bsz = c.shape[0]
    tn = D_MODEL
    return pl.pallas_call(
        _mod_kernel,
        grid=(depth, n_out // tn),
        in_specs=[
            pl.BlockSpec((bsz, d), lambda l, j: (0, 0)),
            pl.BlockSpec((1, d, tn), lambda l, j: (l, 0, j)),
            pl.BlockSpec((1, 1, tn), lambda l, j: (l, 0, j)),
        ],
        out_specs=pl.BlockSpec((1, bsz, tn), lambda l, j: (l, 0, j)),
        out_shape=jax.ShapeDtypeStruct((depth, bsz, n_out), F32),
        compiler_params=_compiler_params(2),
        name="adaln_modulation",
    )(c, mod_w, mod_b.reshape(depth, 1, n_out))


_A_COLS = 3 * SC_WIDTH
_QKV_COLS = 3 * SB_WIDTH
_OFF_QKV = _A_COLS
_OFF_Z = _OFF_QKV + _QKV_COLS
_OFF_XBC = _OFF_Z + SSM_INNER
_OFF_DT = _OFF_XBC + SSM_CONV_DIM
_W1_COLS = _OFF_DT + LANES


def _inproj_kernel(tiles_per_seq, x_ref, mod_ref, g_ref, w_ref, cw_ref,
                   fa_ref, qkv_ref, z_ref, xbc_ref, dt_ref, ubuf):
    i = pl.program_id(0)
    tm = x_ref.shape[0]
    shift = mod_ref[0, 0:1, :]
    scale = mod_ref[0, 1:2, :]

    @pl.when(i % tiles_per_seq == 0)
    def _():
        ubuf[0:HALO, :] = jnp.zeros((HALO, SC_WIDTH), F32)

    @pl.when(i % tiles_per_seq != 0)
    def _():
        ubuf[0:HALO, :] = ubuf[tm:tm + HALO, :]

    for rs in _sub_tiles(tm):
        h = _modulated_prenorm(x_ref[rs, :], g_ref[...], scale, shift).astype(BF16)

        def proj(lo, hi):
            return jnp.dot(h, w_ref[:, lo:hi], preferred_element_type=F32)

        qkv_ref[rs, :] = proj(_OFF_QKV, _OFF_Z).astype(BF16)
        z_ref[rs, :] = proj(_OFF_Z, _OFF_XBC).astype(BF16)
        xbc_ref[rs, :] = proj(_OFF_XBC, _OFF_DT).astype(BF16)
        dt_ref[rs, :] = proj(_OFF_DT, _W1_COLS)

        pa = proj(0, _A_COLS)
        u = pa[:, SC_WIDTH:2 * SC_WIDTH] * pa[:, 2 * SC_WIDTH:]
        ubuf[HALO + rs.start:HALO + rs.stop, :] = u
        conv = cw_ref[SC_KERNEL - 1:SC_KERNEL, :] * u
        for k in range(SC_KERNEL - 1):
            back = SC_KERNEL - 1 - k
            conv = conv + cw_ref[k:k + 1, :] * ubuf[HALO - back + rs.start:HALO - back + rs.stop, :]
        fa_ref[rs, :] = (pa[:, :SC_WIDTH] * conv).astype(BF16)


def _inproj(xf, mod_l, g_pre, w1, sc_conv_w, seq_len):
    n, d = xf.shape
    tm = min(TOKEN_TILE, seq_len)
    tiles_per_seq = seq_len // tm
    row = lambda i: (i, 0)
    return pl.pallas_call(
        functools.partial(_inproj_kernel, tiles_per_seq),
        grid=(n // tm,),
        in_specs=[
            pl.BlockSpec((tm, d), row),
            pl.BlockSpec((1, N_MOD, d), lambda i: (i // tiles_per_seq, 0, 0)),
            _resident((1, d)),
            _resident((d, _W1_COLS)),
            _resident((SC_KERNEL, SC_WIDTH)),
        ],
        out_specs=[
            pl.BlockSpec((tm, SC_WIDTH), row),
            pl.BlockSpec((tm, _QKV_COLS), row),
            pl.BlockSpec((tm, SSM_INNER), row),
            pl.BlockSpec((tm, SSM_CONV_DIM), row),
            pl.BlockSpec((tm, LANES), row),
        ],
        out_shape=[
            jax.ShapeDtypeStruct((n, SC_WIDTH), BF16),
            jax.ShapeDtypeStruct((n, _QKV_COLS), BF16),
            jax.ShapeDtypeStruct((n, SSM_INNER), BF16),
            jax.ShapeDtypeStruct((n, SSM_CONV_DIM), BF16),
            jax.ShapeDtypeStruct((n, LANES), F32),
        ],
        scratch_shapes=[pltpu.VMEM((tm + HALO, SC_WIDTH), F32)],
        compiler_params=_compiler_params(1),
        name="inproj_shortconv",
    )(xf, mod_l, g_pre, w1, sc_conv_w)


def _attn_kernel(q_ref, k_ref, v_ref, o_ref, acc, carry, lb_buf, sp_buf):
    blk = SB_BLOCK
    first_blk = pl.program_id(2) * ATTN_Q_BLOCKS
    q = q_ref[...]

    lane_k = lax.broadcasted_iota(jnp.int32, (blk, LANES), 1)
    low_half = lane_k < SB_HEAD_DIM
    jj = lax.broadcasted_iota(jnp.int32, (2 * blk, 2 * blk), 0)
    ss = lax.broadcasted_iota(jnp.int32, (2 * blk, 2 * blk), 1)
    same_head = (jj // blk) == (ss // blk)
    later_mat = jnp.where(same_head & (jj > ss), 1.0, 0.0).astype(BF16)
    total_mat = jnp.where(same_head, 1.0, 0.0).astype(BF16)
    diag_row = lax.broadcasted_iota(jnp.int32, (blk, 2 * blk), 0)
    diag_key = lax.broadcasted_iota(jnp.int32, (blk, 2 * blk), 1) % blk
    causal = diag_key < diag_row

    def stacked(ref, j):
        t = ref[pl.ds(pl.multiple_of(j * blk, blk), blk), :]
        zero = jnp.zeros_like(t)
        return jnp.concatenate([jnp.where(low_half, t, zero), jnp.where(low_half, zero, t)], axis=0)

    def mask_diagonal(t):
        top = jnp.where(causal, t[:blk], 0.0)
        return top if t.shape[0] == blk else jnp.concatenate([top, t[blk:]], axis=0)

    def softplus_terms(j, r0):
        z = lax.dot_general(q[r0:], stacked(k_ref, j), (((1,), (1,)), ((), ())), preferred_element_type=F32)
        neg_abs = pltpu.bitcast(pltpu.bitcast(z, jnp.uint32) | jnp.uint32(0x80000000), F32)
        sp = jnp.maximum(z, 0.0) + jnp.log2(1.0 + jnp.exp2(neg_abs))
        return z - sp, sp

    def attend(j, r0, log_beta, sp16, diagonal):
        wp = jnp.dot(sp16, later_mat, preferred_element_type=F32)
        tp = jnp.dot(sp16, total_mat, preferred_element_type=F32)
        c = carry[r0:, :]
        att = jnp.exp2(log_beta - (wp + c))
        if diagonal:
            att = mask_diagonal(att)
        carry[r0:, :] = c + tp
        acc[r0:, :] += jnp.dot(att.astype(BF16), stacked(v_ref, j), preferred_element_type=F32)

    def step(j, r0, diagonal):
        log_beta, sp = softplus_terms(j, r0)
        if diagonal:
            sp = mask_diagonal(sp)
        attend(j, r0, log_beta, sp.astype(BF16), diagonal)

    def stage(j):
        log_beta, sp = softplus_terms(j, 0)
        lb_buf[...] = log_beta
        sp_buf[...] = sp.astype(BF16)

    carry[...] = jnp.zeros_like(carry)
    acc[...] = jnp.zeros_like(acc)
    for kb in range(ATTN_Q_BLOCKS - 1, -1, -1):
        step(first_blk + kb, kb * blk, True)

    stage(jnp.maximum(first_blk - 1, 0))

    def body(t, _):
        j = first_blk - 1 - t * ATTN_UNROLL
        attend(j, 0, lb_buf[...], sp_buf[...], False)
        for u in range(1, ATTN_UNROLL):
            step(j - u, 0, False)
        stage(jnp.maximum(j - ATTN_UNROLL, 0))
        return 0

    lax.fori_loop(0, first_blk // ATTN_UNROLL, body, 0)
    o_ref[...] = acc[...].astype(BF16)


def _attention(qkv, bsz, seq_len):
    n = qkv.shape[0]
    rows = ATTN_Q_BLOCKS * SB_BLOCK
    steps = seq_len // rows
    pairs = SB_WIDTH // LANES
    return pl.pallas_call(
        _attn_kernel,
        grid=(bsz, pairs, steps),
        in_specs=[
            pl.BlockSpec((rows, LANES), lambda b, p, i: (b * steps + i, p)),
            pl.BlockSpec((seq_len, LANES), lambda b, p, i: (b, pairs + p)),
            pl.BlockSpec((seq_len, LANES), lambda b, p, i: (b, 2 * pairs + p)),
        ],
        out_specs=pl.BlockSpec((rows, LANES), lambda b, p, i: (b * steps + i, p)),
        out_shape=jax.ShapeDtypeStruct((n, SB_WIDTH), BF16),
        scratch_shapes=[pltpu.VMEM((rows, LANES), F32), pltpu.VMEM((rows, 2 * SB_BLOCK), F32),
                        pltpu.VMEM((rows, 2 * SB_BLOCK), F32), pltpu.VMEM((rows, 2 * SB_BLOCK), BF16)],
        compiler_params=_compiler_params(3),
        name="stickbreak_attention",
    )(qkv, qkv, qkv)


def _ssd_chunks(first, z_ref, xbc_ref, dt_ref, cw_ref, cb_ref, dtb_ref, alog_ref, dskip_ref, nw_ref,
                y_ref, xbuf, state, after_chunk):
    lc = SSM_CHUNK
    rows = z_ref.shape[0]

    @pl.when(first)
    def _():
        xbuf[0:HALO, :] = jnp.zeros((HALO, SSM_CONV_DIM), F32)
        state[...] = jnp.zeros_like(state)

    @pl.when(jnp.logical_not(first))
    def _():
        xbuf[0:HALO, :] = xbuf[rows:rows + HALO, :]

    xbuf[HALO:, :] = xbc_ref[...].astype(F32)
    a_neg = -jnp.exp(alog_ref[...])
    ri = lax.broadcasted_iota(jnp.int32, (lc, lc), 0)
    ci = lax.broadcasted_iota(jnp.int32, (lc, lc), 1)
    lower = ri >= ci
    tril = jnp.where(lower, 1.0, 0.0)
    lane = lax.broadcasted_iota(jnp.int32, (lc, LANES), 1)
    low_half = lane < SSM_HEAD_DIM

    def expand(col_src):
        parts = []
        for p in range(SSM_HEADS // 2):
            lo = jnp.broadcast_to(col_src[:, 2 * p:2 * p + 1], (lc, LANES))
            hi = jnp.broadcast_to(col_src[:, 2 * p + 1:2 * p + 2], (lc, LANES))
            parts.append(jnp.where(low_half, lo, hi))
        return jnp.concatenate(parts, axis=1)

    for r0 in range(0, rows, lc):
        conv = cb_ref[...] + cw_ref[SSM_CONV - 1:SSM_CONV, :] * xbuf[HALO + r0:HALO + r0 + lc, :]
        for k in range(SSM_CONV - 1):
            back = SSM_CONV - 1 - k
            conv = conv + cw_ref[k:k + 1, :] * xbuf[HALO - back + r0:HALO - back + r0 + lc, :]
        xc = conv * _sigmoid(conv)
        xs = xc[:, :SSM_INNER]
        bm16 = xc[:, SSM_INNER:SSM_INNER + LANES].astype(BF16)
        cm16 = xc[:, SSM_INNER + LANES:].astype(BF16)

        dt = _softplus(dt_ref[r0:r0 + lc, :] + dtb_ref[...])
        a = dt * a_neg
        a_cs = jnp.dot(tril, a, preferred_element_type=F32,
                       precision=lax.Precision.HIGHEST)
        a_cs_t = a_cs.T

        dt_x = expand(dt)
        acs_x = expand(a_cs)
        xdt = xs * dt_x
        last_x = acs_x[lc - 1:lc, :]
        decay_to_end = jnp.exp(last_x - acs_x)
        decay_from_start = jnp.exp(acs_x)
        chunk_decay = jnp.exp(last_x)

        cb = []
        for g in range(SSM_GROUPS):
            cg = jnp.where(low_half if g == 0 else ~low_half, cm16, jnp.zeros_like(cm16))
            cb.append(lax.dot_general(cg, bm16, (((1,), (1,)), ((), ())), preferred_element_type=F32))
        xdt16 = xdt.astype(BF16)
        y_pairs = []
        for p in range(SSM_HEADS // 2):
            xp = xdt16[:, p * LANES:(p + 1) * LANES]
            yp = None
            for half in range(2):
                hd = 2 * p + half
                g = hd // (SSM_HEADS // SSM_GROUPS)
                diff = a_cs[:, hd:hd + 1] - a_cs_t[hd:hd + 1, :]
                dec = jnp.exp(jnp.where(lower, diff, -jnp.inf))
                scores = (cb[g] * dec).astype(BF16)
                xh = jnp.where(low_half if half == 0 else ~low_half, xp, jnp.zeros_like(xp))
                t = jnp.dot(scores, xh, preferred_element_type=F32)
                yp = t if yp is None else yp + t
            y_pairs.append(yp)
        y = jnp.concatenate(y_pairs, axis=1)

        prev = state[...]
        y = y + jnp.dot(cm16, prev.astype(BF16), preferred_element_type=F32) * decay_from_start
        xw = (xdt * decay_to_end).astype(BF16)
        cs = lax.dot_general(bm16, xw, (((0,), (0,)), ((), ())), preferred_element_type=F32)
        srow = lax.broadcasted_iota(jnp.int32, cs.shape, 0) // SSM_STATE
        scol = lax.broadcasted_iota(jnp.int32, cs.shape, 1) // (SSM_INNER // SSM_GROUPS)
        state[...] = prev * chunk_decay + jnp.where(srow == scol, cs, 0.0)

        y = y + xs * dskip_ref[...]
        zf = z_ref[r0:r0 + lc, :].astype(F32)
        y = y * (zf * _sigmoid(zf))
        gw = SSM_INNER // SSM_GROUPS
        outs = []
        for g in range(SSM_GROUPS):
            yg = y[:, g * gw:(g + 1) * gw]
            ms = jnp.mean(yg * yg, axis=-1, keepdims=True)
            outs.append(yg * lax.rsqrt(ms + NORM_EPS))
        y_ref[r0:r0 + lc, :] = (jnp.concatenate(outs, axis=1) * nw_ref[...]).astype(BF16)
        after_chunk(r0 + lc)


def _ssd_merge_kernel(tiles_per_seq, x_ref, mod_ref, gpre_ref, gpost_ref, fa_ref, fb_ref,
                      z_ref, xbc_ref, dt_ref, cw_ref, cb_ref, dtb_ref, alog_ref, dskip_ref, nw_ref,
                      wg_ref, wa_ref, wb_ref, wc_ref, wo_ref, o_ref, fc_buf, xbuf, state):
    d = D_MODEL
    shift = mod_ref[0, 0:1, :]
    scale = mod_ref[0, 1:2, :]
    gate = mod_ref[0, 2:3, :]
    first = pl.program_id(0) % tiles_per_seq == 0
    tiles = _sub_tiles(x_ref.shape[0])

    def merge_tile(rs):
        x = x_ref[rs, :]
        h = _modulated_prenorm(x, gpre_ref[...], scale, shift).astype(BF16)
        merged = None
        for br, (f_ref, w_ref) in enumerate(((fa_ref, wa_ref), (fb_ref, wb_ref), (fc_buf, wc_ref))):
            gl = jnp.dot(h, wg_ref[:, br * d:(br + 1) * d], preferred_element_type=F32)
            yb = jnp.dot(f_ref[rs, :], w_ref[...], preferred_element_type=F32)
            t = _sigmoid(gl) * yb
            merged = t if merged is None else merged + t
        mix = jnp.dot(merged.astype(BF16), wo_ref[...], preferred_element_type=F32)
        ms = jnp.mean(mix * mix, axis=-1, keepdims=True)
        o_ref[rs, :] = x + gate * (mix * lax.rsqrt(ms + NORM_EPS) * gpost_ref[...])

    def after_chunk(done_rows):
        for rs in tiles:
            if rs.stop == done_rows:
                merge_tile(rs)

    _ssd_chunks(first, z_ref, xbc_ref, dt_ref, cw_ref, cb_ref, dtb_ref, alog_ref, dskip_ref, nw_ref,
                fc_buf, xbuf, state, after_chunk)


def _ssd_merge(xf, mod_l, g_pre, g_post, fa, fb, z, xbc, dt, conv_w, conv_b, dt_bias, a_log, d_skip, norm_w,
               wg, wa, wb, wc, wo, seq_len):
    n, d = xf.shape
    tm = min(TOKEN_TILE, seq_len)
    assert tm % SSM_CHUNK == 0 and min(SUB_TILE, tm) % SSM_CHUNK == 0
    tiles_per_seq = seq_len // tm
    row = lambda i: (i, 0)
    return pl.pallas_call(
        functools.partial(_ssd_merge_kernel, tiles_per_seq),
        grid=(n // tm,),
        in_specs=[
            pl.BlockSpec((tm, d), row),
            pl.BlockSpec((1, N_MOD, d), lambda i: (i // tiles_per_seq, 0, 0)),
            _resident((1, d)),
            _resident((1, d)),
            pl.BlockSpec((tm, SC_WIDTH), row),
            pl.BlockSpec((tm, SB_WIDTH), row),
            pl.BlockSpec((tm, SSM_INNER), row),
            pl.BlockSpec((tm, SSM_CONV_DIM), row),
            pl.BlockSpec((tm, LANES), row),
            _resident((SSM_CONV, SSM_CONV_DIM)),
            _resident((1, SSM_CONV_DIM)),
            _resident((1, LANES)),
            _resident((1, LANES)),
            _resident((1, SSM_INNER)),
            _resident((1, SSM_INNER)),
            _resident((d, 3 * d)),
            _resident((SC_WIDTH, d)),
            _resident((SB_WIDTH, d)),
            _resident((SSM_INNER, d)),
            _resident((d, d)),
        ],
        out_specs=pl.BlockSpec((tm, d), row),
        out_shape=jax.ShapeDtypeStruct((n, d), F32),
        scratch_shapes=[pltpu.VMEM((tm, SSM_INNER), BF16),
                        pltpu.VMEM((tm + HALO, SSM_CONV_DIM), F32),
                        pltpu.VMEM((SSM_GROUPS * SSM_STATE, SSM_INNER), F32)],
        compiler_params=_compiler_params(1),
        name="ssd_merge",
    )(xf, mod_l, g_pre, g_post, fa, fb, z, xbc, dt, conv_w, conv_b, dt_bias, a_log, d_skip, norm_w,
      wg, wa, wb, wc, wo)


def _ffn_kernel(x_ref, mod_ref, gpre_ref, gpost_ref, win_ref, wout_ref, o_ref):
    shift = mod_ref[0, 3:4, :]
    scale = mod_ref[0, 4:5, :]
    gate = mod_ref[0, 5:6, :]
    for rs in _sub_tiles(x_ref.shape[0]):
        x = x_ref[rs, :]
        h = _modulated_prenorm(x, gpre_ref[...], scale, shift).astype(BF16)
        gt = jnp.dot(h, win_ref[:, :FFN_HIDDEN], preferred_element_type=F32)
        up = jnp.dot(h, win_ref[:, FFN_HIDDEN:], preferred_element_type=F32)
        act = (gt * _sigmoid(gt) * up).astype(BF16)
        f = jnp.dot(act, wout_ref[...], preferred_element_type=F32)
        ms = jnp.mean(f * f, axis=-1, keepdims=True)
        o_ref[rs, :] = x + gate * (f * lax.rsqrt(ms + NORM_EPS) * gpost_ref[...])


def _ffn(xf, mod_l, g_pre, g_post, w_in, w_out, seq_len):
    n, d = xf.shape
    tm = min(TOKEN_TILE, seq_len)
    tiles_per_seq = seq_len // tm
    row = lambda i: (i, 0)
    return pl.pallas_call(
        _ffn_kernel,
        grid=(n // tm,),
        in_specs=[
            pl.BlockSpec((tm, d), row),
            pl.BlockSpec((1, N_MOD, d), lambda i: (i // tiles_per_seq, 0, 0)),
            _resident((1, d)),
            _resident((1, d)),
            _resident((d, 2 * FFN_HIDDEN)),
            _resident((FFN_HIDDEN, d)),
        ],
        out_specs=pl.BlockSpec((tm, d), row),
        out_shape=jax.ShapeDtypeStruct((n, d), F32),
        compiler_params=_compiler_params(1),
        name="swiglu_ffn",
    )(xf, mod_l, g_pre, g_post, w_in, w_out)


def _pad_lanes(v):
    return jnp.pad(v.astype(F32), (0, LANES - v.shape[0])).reshape(1, LANES)


def kernel(x, c, mod_w, mod_b, g_pre_mix, g_post_mix, g_pre_ffn, g_post_ffn, w_in, sc_conv_w,
           ssm_conv_w, ssm_conv_b, ssm_dt_bias, ssm_a_log, ssm_d, ssm_norm_w, w_sc_out, w_sb_out,
           w_ssm_out, w_o, w_ffn_in, w_ffn_out):
    bsz, seq_len, d = x.shape
    depth = mod_w.shape[0]
    assert d == D_MODEL and seq_len % (ATTN_Q_BLOCKS * SB_BLOCK) == 0 and seq_len % SSM_CHUNK == 0
    mod = _modulation(c, mod_w, mod_b).reshape(depth, bsz, N_MOD, d)
    xf = x.reshape(bsz * seq_len, d)
    q_scale = SB_HEAD_DIM ** -0.5 * LOG2_E
    for l in range(depth):
        wl = w_in[l]
        w1 = jnp.concatenate([
            wl[:, :_OFF_QKV],
            wl[:, _OFF_QKV:_OFF_QKV + SB_WIDTH] * q_scale,
            wl[:, _OFF_QKV + SB_WIDTH:_OFF_DT],
            jnp.pad(wl[:, _OFF_DT:_OFF_DT + SSM_HEADS], ((0, 0), (0, LANES - SSM_HEADS))),
        ], axis=1).astype(BF16)
        wg = wl[:, _OFF_DT + SSM_HEADS:].astype(BF16)
        row = lambda v: v.reshape(1, -1).astype(F32)

        fa, qkv, z, xbc, dt = _inproj(xf, mod[l], row(g_pre_mix[l]), w1, sc_conv_w[l].astype(F32), seq_len)
        fb = _attention(qkv, bsz, seq_len)
        xf = _ssd_merge(xf, mod[l], row(g_pre_mix[l]), row(g_post_mix[l]), fa, fb, z, xbc, dt,
                        ssm_conv_w[l].astype(F32), row(ssm_conv_b[l]), _pad_lanes(ssm_dt_bias[l]),
                        _pad_lanes(ssm_a_log[l]), row(jnp.repeat(ssm_d[l], SSM_HEAD_DIM)), row(ssm_norm_w[l]),
                        wg, w_sc_out[l].astype(BF16), w_sb_out[l].astype(BF16), w_ssm_out[l].astype(BF16),
                        w_o[l].astype(BF16), seq_len)
        xf = _ffn(xf, mod[l], row(g_pre_ffn[l]), row(g_post_ffn[l]), w_ffn_in[l].astype(BF16),
                  w_ffn_out[l].astype(BF16), seq_len)
    return xf.reshape(bsz, seq_len, d)
```

```python
import functools

import jax
import jax.numpy as jnp
from jax import lax
from jax.experimental import pallas as pl
from jax.experimental.pallas import tpu as pltpu

D_MODEL = 1024
SC_WIDTH = 256
SC_KERNEL = 3
SB_HEAD_DIM = 64
SB_WIDTH = 256
SB_BLOCK = 128
SSM_INNER = 512
SSM_HEAD_DIM = 64
SSM_HEADS = 8
SSM_GROUPS = 2
SSM_STATE = 64
SSM_CONV = 4
SSM_CHUNK = 256
SSM_CONV_DIM = SSM_INNER + 2 * SSM_GROUPS * SSM_STATE
FFN_HIDDEN = 2816
NORM_EPS = 1e-6
LOG2_E = 1.4426950408889634
N_MOD = 6

LANES = 128
SUBLANES = 8
VMEM_LIMIT_BYTES = 56 * 1024 * 1024

TOKEN_TILE = 1024
SUB_TILE = 512
ATTN_Q_BLOCKS = 8
ATTN_UNROLL = 8
HALO = SUBLANES

F32 = jnp.float32
BF16 = jnp.bfloat16


def _compiler_params(n_axes):
    return pltpu.CompilerParams(
        dimension_semantics=("arbitrary",) * n_axes,
        vmem_limit_bytes=VMEM_LIMIT_BYTES,
    )


def _resident(block_shape):
    zeros = (0,) * len(block_shape)
    return pl.BlockSpec(block_shape, lambda *_: zeros, pipeline_mode=pl.Buffered(1))


def _sub_tiles(rows):
    sub = min(SUB_TILE, rows)
    assert rows % sub == 0
    return [slice(r, r + sub) for r in range(0, rows, sub)]


def _sigmoid(v):
    return 1.0 / (1.0 + jnp.exp(-v))


def _softplus(v):
    return jnp.maximum(v, 0.0) + jnp.log(1.0 + jnp.exp(-jnp.abs(v)))


def _modulated_prenorm(x, g, scale, shift):
    ms = jnp.mean(x * x, axis=-1, keepdims=True)
    return x * lax.rsqrt(ms + NORM_EPS) * (g * (1.0 + scale)) + shift


def _mod_kernel(c_ref, w_ref, b_ref, o_ref):
    c = c_ref[...]
    sc = c * _sigmoid(c)
    o_ref[0] = jnp.dot(sc, w_ref[0], preferred_element_type=F32,
                       precision=lax.Precision.HIGHEST) + b_ref[0]


def _modulation(c, mod_w, mod_b):
    depth, d, n_out = mod_w.shape
    bsz = c.shape[0]
    tn = D_MODEL
    return pl.pallas_call(
        _mod_kernel,
        grid=(depth, n_out // tn),
        in_specs=[
            pl.BlockSpec((bsz, d), lambda l, j: (0, 0)),
            pl.BlockSpec((1, d, tn), lambda l, j: (l, 0, j)),
            pl.BlockSpec((1, 1, tn), lambda l, j: (l, 0, j)),
        ],
        out_specs=pl.BlockSpec((1, bsz, tn), lambda l, j: (l, 0, j)),
        out_shape=jax.ShapeDtypeStruct((depth, bsz, n_out), F32),
        compiler_params=_compiler_params(2),
        name="adaln_modulation",
    )(c, mod_w, mod_b.reshape(depth, 1, n_out))


_A_COLS = 3 * SC_WIDTH
_QKV_COLS = 3 * SB_WIDTH
_OFF_QKV = _A_COLS
_OFF_Z = _OFF_QKV + _QKV_COLS
_OFF_XBC = _OFF_Z + SSM_INNER
_OFF_DT = _OFF_XBC + SSM_CONV_DIM
_W1_COLS = _OFF_DT + LANES


def _inproj_kernel(tiles_per_seq, x_ref, mod_ref, g_ref, w_ref, cw_ref,
                   fa_ref, qkv_ref, z_ref, xbc_ref, dt_ref, ubuf):
    i = pl.program_id(0)
    tm = x_ref.shape[0]
    shift = mod_ref[0, 0:1, :]
    scale = mod_ref[0, 1:2, :]

    @pl.when(i % tiles_per_seq == 0)
    def _():
        ubuf[0:HALO, :] = jnp.zeros((HALO, SC_WIDTH), F32)

    @pl.when(i % tiles_per_seq != 0)
    def _():
        ubuf[0:HALO, :] = ubuf[tm:tm + HALO, :]

    for rs in _sub_tiles(tm):
        h = _modulated_prenorm(x_ref[rs, :], g_ref[...], scale, shift).astype(BF16)

        def proj(lo, hi):
            return jnp.dot(h, w_ref[:, lo:hi], preferred_element_type=F32)

        qkv_ref[rs, :] = proj(_OFF_QKV, _OFF_Z).astype(BF16)
        z_ref[rs, :] = proj(_OFF_Z, _OFF_XBC).astype(BF16)
        xbc_ref[rs, :] = proj(_OFF_XBC, _OFF_DT).astype(BF16)
        dt_ref[rs, :] = proj(_OFF_DT, _W1_COLS)

        pa = proj(0, _A_COLS)
        u = pa[:, SC_WIDTH:2 * SC_WIDTH] * pa[:, 2 * SC_WIDTH:]
        ubuf[HALO + rs.start:HALO + rs.stop, :] = u
        conv = cw_ref[SC_KERNEL - 1:SC_KERNEL, :] * u
        for k in range(SC_KERNEL - 1):
            back = SC_KERNEL - 1 - k
            conv = conv + cw_ref[k:k + 1, :] * ubuf[HALO - back + rs.start:HALO - back + rs.stop, :]
        fa_ref[rs, :] = (pa[:, :SC_WIDTH] * conv).astype(BF16)


def _inproj(xf, mod_l, g_pre, w1, sc_conv_w, seq_len):
    n, d = xf.shape
    tm = min(TOKEN_TILE, seq_len)
    tiles_per_seq = seq_len // tm
    row = lambda i: (i, 0)
    return pl.pallas_call(
        functools.partial(_inproj_kernel, tiles_per_seq),
        grid=(n // tm,),
        in_specs=[
            pl.BlockSpec((tm, d), row),
            pl.BlockSpec((1, N_MOD, d), lambda i: (i // tiles_per_seq, 0, 0)),
            _resident((1, d)),
            _resident((d, _W1_COLS)),
            _resident((SC_KERNEL, SC_WIDTH)),
        ],
        out_specs=[
            pl.BlockSpec((tm, SC_WIDTH), row),
            pl.BlockSpec((tm, _QKV_COLS), row),
            pl.BlockSpec((tm, SSM_INNER), row),
            pl.BlockSpec((tm, SSM_CONV_DIM), row),
            pl.BlockSpec((tm, LANES), row),
        ],
        out_shape=[
            jax.ShapeDtypeStruct((n, SC_WIDTH), BF16),
            jax.ShapeDtypeStruct((n, _QKV_COLS), BF16),
            jax.ShapeDtypeStruct((n, SSM_INNER), BF16),
            jax.ShapeDtypeStruct((n, SSM_CONV_DIM), BF16),
            jax.ShapeDtypeStruct((n, LANES), F32),
        ],
        scratch_shapes=[pltpu.VMEM((tm + HALO, SC_WIDTH), F32)],
        compiler_params=_compiler_params(1),
        name="inproj_shortconv",
    )(xf, mod_l, g_pre, w1, sc_conv_w)


def _attn_kernel(q_ref, k_ref, v_ref, o_ref, acc, lb_buf, sp_buf):
    blk = SB_BLOCK
    first_blk = pl.program_id(2) * ATTN_Q_BLOCKS
    last_blk = k_ref.shape[0] // blk - 1
    q = q_ref[...]

    lane_k = lax.broadcasted_iota(jnp.int32, (blk, LANES), 1)
    low_half = lane_k < SB_HEAD_DIM
    jj = lax.broadcasted_iota(jnp.int32, (2 * blk, 2 * blk), 0)
    ss = lax.broadcasted_iota(jnp.int32, (2 * blk, 2 * blk), 1)
    later_mat = jnp.where(((jj // blk) == (ss // blk)) & (jj > ss), 1.0, 0.0).astype(BF16)
    kh = lax.broadcasted_iota(jnp.int32, (2 * blk, LANES), 0) // blk
    oh = lax.broadcasted_iota(jnp.int32, (2 * blk, LANES), 1) // SB_HEAD_DIM
    neg_total_mat = jnp.where(kh == oh, -1.0, 0.0).astype(BF16)
    diag_row = lax.broadcasted_iota(jnp.int32, (blk, 2 * blk), 0)
    diag_key = lax.broadcasted_iota(jnp.int32, (blk, 2 * blk), 1) % blk
    causal = diag_key < diag_row

    def stacked(ref, j):
        t = ref[pl.ds(pl.multiple_of(j * blk, blk), blk), :]
        zero = jnp.zeros_like(t)
        return jnp.concatenate([jnp.where(low_half, t, zero), jnp.where(low_half, zero, t)], axis=0)

    def mask_diagonal(t):
        top = jnp.where(causal, t[:blk], 0.0)
        return top if t.shape[0] == blk else jnp.concatenate([top, t[blk:]], axis=0)

    def softplus_terms(j, r0):
        z = lax.dot_general(q[r0:], stacked(k_ref, j), (((1,), (1,)), ((), ())), preferred_element_type=F32)
        neg_abs = pltpu.bitcast(pltpu.bitcast(z, jnp.uint32) | jnp.uint32(0x80000000), F32)
        sp = jnp.maximum(z, 0.0) + jnp.log2(1.0 + jnp.exp2(neg_abs))
        return z - sp, sp

    def attend(j, r0, log_beta, sp16, diagonal):
        wp = jnp.dot(sp16, later_mat, preferred_element_type=F32)
        neg_tot = jnp.dot(sp16, neg_total_mat, preferred_element_type=F32)
        att = jnp.exp2(log_beta - wp)
        if diagonal:
            att = mask_diagonal(att)
        acc[r0:, :] = acc[r0:, :] * jnp.exp2(neg_tot) + jnp.dot(att.astype(BF16), stacked(v_ref, j),
                                                                preferred_element_type=F32)

    def step(j, r0, diagonal):
        log_beta, sp = softplus_terms(j, r0)
        if diagonal:
            sp = mask_diagonal(sp)
        attend(j, r0, log_beta, sp.astype(BF16), diagonal)

    def stage(j):
        log_beta, sp = softplus_terms(j, 0)
        lb_buf[...] = log_beta
        sp_buf[...] = sp.astype(BF16)

    acc[...] = jnp.zeros_like(acc)
    stage(0)

    def body(t, _):
        j = t * ATTN_UNROLL
        attend(j, 0, lb_buf[...], sp_buf[...], False)
        for u in range(1, ATTN_UNROLL):
            step(j + u, 0, False)
        stage(jnp.minimum(j + ATTN_UNROLL, last_blk))
        return 0

    lax.fori_loop(0, first_blk // ATTN_UNROLL, body, 0)
    for kb in range(ATTN_Q_BLOCKS):
        step(first_blk + kb, kb * blk, True)
    o_ref[...] = acc[...].astype(BF16)


def _attention(qkv, bsz, seq_len):
    n = qkv.shape[0]
    rows = ATTN_Q_BLOCKS * SB_BLOCK
    steps = seq_len // rows
    pairs = SB_WIDTH // LANES
    return pl.pallas_call(
        _attn_kernel,
        grid=(bsz, pairs, steps),
        in_specs=[
            pl.BlockSpec((rows, LANES), lambda b, p, i: (b * steps + i, p)),
            pl.BlockSpec((seq_len, LANES), lambda b, p, i: (b, pairs + p)),
            pl.BlockSpec((seq_len, LANES), lambda b, p, i: (b, 2 * pairs + p)),
        ],
        out_specs=pl.BlockSpec((rows, LANES), lambda b, p, i: (b * steps + i, p)),
        out_shape=jax.ShapeDtypeStruct((n, SB_WIDTH), BF16),
        scratch_shapes=[pltpu.VMEM((rows, LANES), F32),
                        pltpu.VMEM((rows, 2 * SB_BLOCK), F32), pltpu.VMEM((rows, 2 * SB_BLOCK), BF16)],
        compiler_params=_compiler_params(3),
        name="stickbreak_attention",
    )(qkv, qkv, qkv)


def _ssd_chunks(first, z_ref, xbc_ref, dt_ref, cw_ref, cb_ref, dtb_ref, alog_ref, dskip_ref, nw_ref,
                y_ref, xbuf, state, after_chunk):
    lc = SSM_CHUNK
    rows = z_ref.shape[0]

    @pl.when(first)
    def _():
        xbuf[0:HALO, :] = jnp.zeros((HALO, SSM_CONV_DIM), F32)
        state[...] = jnp.zeros_like(state)

    @pl.when(jnp.logical_not(first))
    def _():
        xbuf[0:HALO, :] = xbuf[rows:rows + HALO, :]

    xbuf[HALO:, :] = xbc_ref[...].astype(F32)
    a_neg = -jnp.exp(alog_ref[...])
    ri = lax.broadcasted_iota(jnp.int32, (lc, lc), 0)
    ci = lax.broadcasted_iota(jnp.int32, (lc, lc), 1)
    lower = ri >= ci
    tril = jnp.where(lower, 1.0, 0.0)
    lane = lax.broadcasted_iota(jnp.int32, (lc, LANES), 1)
    low_half = lane < SSM_HEAD_DIM

    def expand(col_src):
        parts = []
        for p in range(SSM_HEADS // 2):
            lo = jnp.broadcast_to(col_src[:, 2 * p:2 * p + 1], (lc, LANES))
            hi = jnp.broadcast_to(col_src[:, 2 * p + 1:2 * p + 2], (lc, LANES))
            parts.append(jnp.where(low_half, lo, hi))
        return jnp.concatenate(parts, axis=1)

    for r0 in range(0, rows, lc):
        conv = cb_ref[...] + cw_ref[SSM_CONV - 1:SSM_CONV, :] * xbuf[HALO + r0:HALO + r0 + lc, :]
        for k in range(SSM_CONV - 1):
            back = SSM_CONV - 1 - k
            conv = conv + cw_ref[k:k + 1, :] * xbuf[HALO - back + r0:HALO - back + r0 + lc, :]
        xc = conv * _sigmoid(conv)
        xs = xc[:, :SSM_INNER]
        bm16 = xc[:, SSM_INNER:SSM_INNER + LANES].astype(BF16)
        cm16 = xc[:, SSM_INNER + LANES:].astype(BF16)

        dt = _softplus(dt_ref[r0:r0 + lc, :] + dtb_ref[...])
        a = dt * a_neg
        a_cs = jnp.dot(tril, a, preferred_element_type=F32,
                       precision=lax.Precision.HIGHEST)
        a_cs_t = a_cs.T

        dt_x = expand(dt)
        acs_x = expand(a_cs)
        xdt = xs * dt_x
        last_x = acs_x[lc - 1:lc, :]
        decay_to_end = jnp.exp(last_x - acs_x)
        decay_from_start = jnp.exp(acs_x)
        chunk_decay = jnp.exp(last_x)

        cb = []
        for g in range(SSM_GROUPS):
            cg = jnp.where(low_half if g == 0 else ~low_half, cm16, jnp.zeros_like(cm16))
            cb.append(lax.dot_general(cg, bm16, (((1,), (1,)), ((), ())), preferred_element_type=F32))
        xdt16 = xdt.astype(BF16)
        y_pairs = []
        for p in range(SSM_HEADS // 2):
            xp = xdt16[:, p * LANES:(p + 1) * LANES]
            yp = None
            for half in range(2):
                hd = 2 * p + half
                g = hd // (SSM_HEADS // SSM_GROUPS)
                diff = a_cs[:, hd:hd + 1] - a_cs_t[hd:hd + 1, :]
                dec = jnp.exp(jnp.where(lower, diff, -jnp.inf))
                scores = (cb[g] * dec).astype(BF16)
                xh = jnp.where(low_half if half == 0 else ~low_half, xp, jnp.zeros_like(xp))
                t = jnp.dot(scores, xh, preferred_element_type=F32)
                yp = t if yp is None else yp + t
            y_pairs.append(yp)
        y = jnp.concatenate(y_pairs, axis=1)

        prev = state[...]
        y = y + jnp.dot(cm16, prev.astype(BF16), preferred_element_type=F32) * decay_from_start
        xw = (xdt * decay_to_end).astype(BF16)
        cs = lax.dot_general(bm16, xw, (((0,), (0,)), ((), ())), preferred_element_type=F32)
        srow = lax.broadcasted_iota(jnp.int32, cs.shape, 0) // SSM_STATE
        scol = lax.broadcasted_iota(jnp.int32, cs.shape, 1) // (SSM_INNER // SSM_GROUPS)
        state[...] = prev * chunk_decay + jnp.where(srow == scol, cs, 0.0)

        y = y + xs * dskip_ref[...]
        zf = z_ref[r0:r0 + lc, :].astype(F32)
        y = y * (zf * _sigmoid(zf))
        gw = SSM_INNER // SSM_GROUPS
        outs = []
        for g in range(SSM_GROUPS):
            yg = y[:, g * gw:(g + 1) * gw]
            ms = jnp.mean(yg * yg, axis=-1, keepdims=True)
            outs.append(yg * lax.rsqrt(ms + NORM_EPS))
        y_ref[r0:r0 + lc, :] = (jnp.concatenate(outs, axis=1) * nw_ref[...]).astype(BF16)
        after_chunk(r0 + lc)


def _ssd_merge_kernel(tiles_per_seq, x_ref, mod_ref, gpre_ref, gpost_ref, fa_ref, fb_ref,
                      z_ref, xbc_ref, dt_ref, cw_ref, cb_ref, dtb_ref, alog_ref, dskip_ref, nw_ref,
                      wg_ref, wa_ref, wb_ref, wc_ref, wo_ref, o_ref, fc_buf, xbuf, state):
    d = D_MODEL
    shift = mod_ref[0, 0:1, :]
    scale = mod_ref[0, 1:2, :]
    gate = mod_ref[0, 2:3, :]
    first = pl.program_id(0) % tiles_per_seq == 0
    tiles = _sub_tiles(x_ref.shape[0])

    def merge_tile(rs):
        x = x_ref[rs, :]
        h = _modulated_prenorm(x, gpre_ref[...], scale, shift).astype(BF16)
        merged = None
        for br, (f_ref, w_ref) in enumerate(((fa_ref, wa_ref), (fb_ref, wb_ref), (fc_buf, wc_ref))):
            gl = jnp.dot(h, wg_ref[:, br * d:(br + 1) * d], preferred_element_type=F32)
            yb = jnp.dot(f_ref[rs, :], w_ref[...], preferred_element_type=F32)
            t = _sigmoid(gl) * yb
            merged = t if merged is None else merged + t
        mix = jnp.dot(merged.astype(BF16), wo_ref[...], preferred_element_type=F32)
        ms = jnp.mean(mix * mix, axis=-1, keepdims=True)
        o_ref[rs, :] = x + gate * (mix * lax.rsqrt(ms + NORM_EPS) * gpost_ref[...])

    def after_chunk(done_rows):
        for rs in tiles:
            if rs.stop == done_rows:
                merge_tile(rs)

    _ssd_chunks(first, z_ref, xbc_ref, dt_ref, cw_ref, cb_ref, dtb_ref, alog_ref, dskip_ref, nw_ref,
                fc_buf, xbuf, state, after_chunk)


def _ssd_merge(xf, mod_l, g_pre, g_post, fa, fb, z, xbc, dt, conv_w, conv_b, dt_bias, a_log, d_skip, norm_w,
               wg, wa, wb, wc, wo, seq_len):
    n, d = xf.shape
    tm = min(TOKEN_TILE, seq_len)
    assert tm % SSM_CHUNK == 0 and min(SUB_TILE, tm) % SSM_CHUNK == 0
    tiles_per_seq = seq_len // tm
    row = lambda i: (i, 0)
    return pl.pallas_call(
        functools.partial(_ssd_merge_kernel, tiles_per_seq),
        grid=(n // tm,),
        in_specs=[
            pl.BlockSpec((tm, d), row),
            pl.BlockSpec((1, N_MOD, d), lambda i: (i // tiles_per_seq, 0, 0)),
            _resident((1, d)),
            _resident((1, d)),
            pl.BlockSpec((tm, SC_WIDTH), row),
            pl.BlockSpec((tm, SB_WIDTH), row),
            pl.BlockSpec((tm, SSM_INNER), row),
            pl.BlockSpec((tm, SSM_CONV_DIM), row),
            pl.BlockSpec((tm, LANES), row),
            _resident((SSM_CONV, SSM_CONV_DIM)),
            _resident((1, SSM_CONV_DIM)),
            _resident((1, LANES)),
            _resident((1, LANES)),
            _resident((1, SSM_INNER)),
            _resident((1, SSM_INNER)),
            _resident((d, 3 * d)),
            _resident((SC_WIDTH, d)),
            _resident((SB_WIDTH, d)),
            _resident((SSM_INNER, d)),
            _resident((d, d)),
        ],
        out_specs=pl.BlockSpec((tm, d), row),
        out_shape=jax.ShapeDtypeStruct((n, d), F32),
        scratch_shapes=[pltpu.VMEM((tm, SSM_INNER), BF16),
                        pltpu.VMEM((tm + HALO, SSM_CONV_DIM), F32),
                        pltpu.VMEM((SSM_GROUPS * SSM_STATE, SSM_INNER), F32)],
        compiler_params=_compiler_params(1),
        name="ssd_merge",
    )(xf, mod_l, g_pre, g_post, fa, fb, z, xbc, dt, conv_w, conv_b, dt_bias, a_log, d_skip, norm_w,
      wg, wa, wb, wc, wo)


def _ffn_kernel(x_ref, mod_ref, gpre_ref, gpost_ref, win_ref, wout_ref, o_ref):
    shift = mod_ref[0, 3:4, :]
    scale = mod_ref[0, 4:5, :]
    gate = mod_ref[0, 5:6, :]
    for rs in _sub_tiles(x_ref.shape[0]):
        x = x_ref[rs, :]
        h = _modulated_prenorm(x, gpre_ref[...], scale, shift).astype(BF16)
        gt = jnp.dot(h, win_ref[:, :FFN_HIDDEN], preferred_element_type=F32)
        up = jnp.dot(h, win_ref[:, FFN_HIDDEN:], preferred_element_type=F32)
        act = (gt * _sigmoid(gt) * up).astype(BF16)
        f = jnp.dot(act, wout_ref[...], preferred_element_type=F32)
        ms = jnp.mean(f * f, axis=-1, keepdims=True)
        o_ref[rs, :] = x + gate * (f * lax.rsqrt(ms + NORM_EPS) * gpost_ref[...])


def _ffn(xf, mod_l, g_pre, g_post, w_in, w_out, seq_len):
    n, d = xf.shape
    tm = min(TOKEN_TILE, seq_len)
    tiles_per_seq = seq_len // tm
    row = lambda i: (i, 0)
    return pl.pallas_call(
        _ffn_kernel,
        grid=(n // tm,),
        in_specs=[
            pl.BlockSpec((tm, d), row),
            pl.BlockSpec((1, N_MOD, d), lambda i: (i // tiles_per_seq, 0, 0)),
            _resident((1, d)),
            _resident((1, d)),
            _resident((d, 2 * FFN_HIDDEN)),
            _resident((FFN_HIDDEN, d)),
        ],
        out_specs=pl.BlockSpec((tm, d), row),
        out_shape=jax.ShapeDtypeStruct((n, d), F32),
        compiler_params=_compiler_params(1),
        name="swiglu_ffn",
    )(xf, mod_l, g_pre, g_post, w_in, w_out)


def _pad_lanes(v):
    return jnp.pad(v.astype(F32), (0, LANES - v.shape[0])).reshape(1, LANES)


def kernel(x, c, mod_w, mod_b, g_pre_mix, g_post_mix, g_pre_ffn, g_post_ffn, w_in, sc_conv_w,
           ssm_conv_w, ssm_conv_b, ssm_dt_bias, ssm_a_log, ssm_d, ssm_norm_w, w_sc_out, w_sb_out,
           w_ssm_out, w_o, w_ffn_in, w_ffn_out):
    bsz, seq_len, d = x.shape
    depth = mod_w.shape[0]
    assert d == D_MODEL and seq_len % (ATTN_Q_BLOCKS * SB_BLOCK) == 0 and seq_len % SSM_CHUNK == 0
    mod = _modulation(c, mod_w, mod_b).reshape(depth, bsz, N_MOD, d)
    xf = x.reshape(bsz * seq_len, d)
    q_scale = SB_HEAD_DIM ** -0.5 * LOG2_E
    for l in range(depth):
        wl = w_in[l]
        w1 = jnp.concatenate([
            wl[:, :_OFF_QKV],
            wl[:, _OFF_QKV:_OFF_QKV + SB_WIDTH] * q_scale,
            wl[:, _OFF_QKV + SB_WIDTH:_OFF_DT],
            jnp.pad(wl[:, _OFF_DT:_OFF_DT + SSM_HEADS], ((0, 0), (0, LANES - SSM_HEADS))),
        ], axis=1).astype(BF16)
        wg = wl[:, _OFF_DT + SSM_HEADS:].astype(BF16)
        row = lambda v: v.reshape(1, -1).astype(F32)

        fa, qkv, z, xbc, dt = _inproj(xf, mod[l], row(g_pre_mix[l]), w1, sc_conv_w[l].astype(F32), seq_len)
        fb = _attention(qkv, bsz, seq_len)
        xf = _ssd_merge(xf, mod[l], row(g_pre_mix[l]), row(g_post_mix[l]), fa, fb, z, xbc, dt,
                        ssm_conv_w[l].astype(F32), row(ssm_conv_b[l]), _pad_lanes(ssm_dt_bias[l]),
                        _pad_lanes(ssm_a_log[l]), row(jnp.repeat(ssm_d[l], SSM_HEAD_DIM)), row(ssm_norm_w[l]),
                        wg, w_sc_out[l].astype(BF16), w_sb_out[l].astype(BF16), w_ssm_out[l].astype(BF16),
                        w_o[l].astype(BF16), seq_len)
        xf = _ffn(xf, mod[l], row(g_pre_ffn[l]), row(g_post_ffn[l]), w_ffn_in[l].astype(BF16),
                  w_ffn_out[l].astype(BF16), seq_len)
    return xf.reshape(bsz, seq_len, d)
```

```python
import functools

import jax
import jax.numpy as jnp
from jax import lax
from jax.experimental import pallas as pl
from jax.experimental.pallas import tpu as pltpu

D_MODEL = 1024
SC_WIDTH = 256
SC_KERNEL = 3
SB_HEAD_DIM = 64
SB_WIDTH = 256
SB_BLOCK = 128
SSM_INNER = 512
SSM_HEAD_DIM = 64
SSM_HEADS = 8
SSM_GROUPS = 2
SSM_STATE = 64
SSM_CONV = 4
SSM_CHUNK = 256
SSM_CONV_DIM = SSM_INNER + 2 * SSM_GROUPS * SSM_STATE
FFN_HIDDEN = 2816
NORM_EPS = 1e-6
LOG2_E = 1.4426950408889634
N_MOD = 6

LANES = 128
SUBLANES = 8
VMEM_LIMIT_BYTES = 56 * 1024 * 1024

TOKEN_TILE = 1024
SUB_TILE = 512
ATTN_Q_BLOCKS = 8
ATTN_UNROLL = 8
HALO = SUBLANES

F32 = jnp.float32
BF16 = jnp.bfloat16


def _compiler_params(n_axes):
    return pltpu.CompilerParams(
        dimension_semantics=("arbitrary",) * n_axes,
        vmem_limit_bytes=VMEM_LIMIT_BYTES,
    )


def _resident(block_shape):
    zeros = (0,) * len(block_shape)
    return pl.BlockSpec(block_shape, lambda *_: zeros, pipeline_mode=pl.Buffered(1))


def _sub_tiles(rows):
    sub = min(SUB_TILE, rows)
    assert rows % sub == 0
    return [slice(r, r + sub) for r in range(0, rows, sub)]


def _sigmoid(v):
    return 1.0 / (1.0 + jnp.exp(-v))


def _softplus(v):
    return jnp.maximum(v, 0.0) + jnp.log(1.0 + jnp.exp(-jnp.abs(v)))


def _modulated_prenorm(x, g, scale, shift):
    ms = jnp.mean(x * x, axis=-1, keepdims=True)
    return x * lax.rsqrt(ms + NORM_EPS) * (g * (1.0 + scale)) + shift


def _mod_kernel(c_ref, w_ref, b_ref, o_ref):
    c = c_ref[...]
    sc = c * _sigmoid(c)
    o_ref[0] = jnp.dot(sc, w_ref[0], preferred_element_type=F32,
                       precision=lax.Precision.HIGHEST) + b_ref[0]


def _modulation(c, mod_w, mod_b):
    depth, d, n_out = mod_w.shape
    bsz = c.shape[0]
    tn = D_MODEL
    return pl.pallas_call(
        _mod_kernel,
        grid=(depth, n_out // tn),
        in_specs=[
            pl.BlockSpec((bsz, d), lambda l, j: (0, 0)),
            pl.BlockSpec((1, d, tn), lambda l, j: (l, 0, j)),
            pl.BlockSpec((1, 1, tn), lambda l, j: (l, 0, j)),
        ],
        out_specs=pl.BlockSpec((1, bsz, tn), lambda l, j: (l, 0, j)),
        out_shape=jax.ShapeDtypeStruct((depth, bsz, n_out), F32),
        compiler_params=_compiler_params(2),
        name="adaln_modulation",
    )(c, mod_w, mod_b.reshape(depth, 1, n_out))


_A_COLS = 3 * SC_WIDTH
_QKV_COLS = 3 * SB_WIDTH
_OFF_QKV = _A_COLS
_OFF_Z = _OFF_QKV + _QKV_COLS
_OFF_XBC = _OFF_Z + SSM_INNER
_OFF_DT = _OFF_XBC + SSM_CONV_DIM
_W1_COLS = _OFF_DT + LANES


def _inproj_kernel(tiles_per_seq, x_ref, mod_ref, g_ref, w32_ref, wscale_ref, cw_ref,
                   fa_ref, qkv_ref, z_ref, xbc_ref, dt_ref, ubuf, w_ref):
    i = pl.program_id(0)
    tm = x_ref.shape[0]

    @pl.when(i == 0)
    def _():
        w_ref[...] = (w32_ref[0] * wscale_ref[...]).astype(BF16)

    shift = mod_ref[0, 0:1, :]
    scale = mod_ref[0, 1:2, :]

    @pl.when(i % tiles_per_seq == 0)
    def _():
        ubuf[0:HALO, :] = jnp.zeros((HALO, SC_WIDTH), F32)

    @pl.when(i % tiles_per_seq != 0)
    def _():
        ubuf[0:HALO, :] = ubuf[tm:tm + HALO, :]

    for rs in _sub_tiles(tm):
        h = _modulated_prenorm(x_ref[rs, :], g_ref[...], scale, shift).astype(BF16)

        def proj(lo, hi):
            return jnp.dot(h, w_ref[:, lo:hi], preferred_element_type=F32)

        qkv_ref[rs, :] = proj(_OFF_QKV, _OFF_Z).astype(BF16)
        z_ref[rs, :] = proj(_OFF_Z, _OFF_XBC).astype(BF16)
        xbc_ref[rs, :] = proj(_OFF_XBC, _OFF_DT).astype(BF16)
        dt_ref[rs, :] = proj(_OFF_DT, _W1_COLS)

        pa = proj(0, _A_COLS)
        u = pa[:, SC_WIDTH:2 * SC_WIDTH] * pa[:, 2 * SC_WIDTH:]
        ubuf[HALO + rs.start:HALO + rs.stop, :] = u
        conv = cw_ref[SC_KERNEL - 1:SC_KERNEL, :] * u
        for k in range(SC_KERNEL - 1):
            back = SC_KERNEL - 1 - k
            conv = conv + cw_ref[k:k + 1, :] * ubuf[HALO - back + rs.start:HALO - back + rs.stop, :]
        fa_ref[rs, :] = (pa[:, :SC_WIDTH] * conv).astype(BF16)


def _inproj(xf, mod_l, g_pre, w_in, layer, col_scale, sc_conv_w, seq_len):
    n, d = xf.shape
    tm = min(TOKEN_TILE, seq_len)
    tiles_per_seq = seq_len // tm
    row = lambda i: (i, 0)
    return pl.pallas_call(
        functools.partial(_inproj_kernel, tiles_per_seq),
        grid=(n // tm,),
        in_specs=[
            pl.BlockSpec((tm, d), row),
            pl.BlockSpec((1, N_MOD, d), lambda i: (i // tiles_per_seq, 0, 0)),
            _resident((1, d)),
            pl.BlockSpec((1, d, _W1_COLS), lambda i: (layer, 0, 0), pipeline_mode=pl.Buffered(1)),
            _resident((1, _W1_COLS)),
            _resident((SC_KERNEL, SC_WIDTH)),
        ],
        out_specs=[
            pl.BlockSpec((tm, SC_WIDTH), row),
            pl.BlockSpec((tm, _QKV_COLS), row),
            pl.BlockSpec((tm, SSM_INNER), row),
            pl.BlockSpec((tm, SSM_CONV_DIM), row),
            pl.BlockSpec((tm, LANES), row),
        ],
        out_shape=[
            jax.ShapeDtypeStruct((n, SC_WIDTH), BF16),
            jax.ShapeDtypeStruct((n, _QKV_COLS), BF16),
            jax.ShapeDtypeStruct((n, SSM_INNER), BF16),
            jax.ShapeDtypeStruct((n, SSM_CONV_DIM), BF16),
            jax.ShapeDtypeStruct((n, LANES), F32),
        ],
        scratch_shapes=[pltpu.VMEM((tm + HALO, SC_WIDTH), F32), pltpu.VMEM((d, _W1_COLS), BF16)],
        compiler_params=_compiler_params(1),
        name="inproj_shortconv",
    )(xf, mod_l, g_pre, w_in, col_scale, sc_conv_w)


def _attn_kernel(q_ref, k_ref, v_ref, o_ref, acc, carry, lb_buf, sp_buf):
    blk = SB_BLOCK
    first_blk = pl.program_id(2) * ATTN_Q_BLOCKS
    q = q_ref[...]

    lane_k = lax.broadcasted_iota(jnp.int32, (blk, LANES), 1)
    low_half = lane_k < SB_HEAD_DIM
    jj = lax.broadcasted_iota(jnp.int32, (2 * blk, 2 * blk), 0)
    ss = lax.broadcasted_iota(jnp.int32, (2 * blk, 2 * blk), 1)
    same_head = (jj // blk) == (ss // blk)
    later_mat = jnp.where(same_head & (jj > ss), 1.0, 0.0).astype(BF16)
    total_mat = jnp.where(same_head, 1.0, 0.0).astype(BF16)
    diag_row = lax.broadcasted_iota(jnp.int32, (blk, 2 * blk), 0)
    diag_key = lax.broadcasted_iota(jnp.int32, (blk, 2 * blk), 1) % blk
    causal = diag_key < diag_row

    def stacked(ref, j):
        t = ref[pl.ds(pl.multiple_of(j * blk, blk), blk), :]
        zero = jnp.zeros_like(t)
        return jnp.concatenate([jnp.where(low_half, t, zero), jnp.where(low_half, zero, t)], axis=0)

    def mask_diagonal(t):
        top = jnp.where(causal, t[:blk], 0.0)
        return top if t.shape[0] == blk else jnp.concatenate([top, t[blk:]], axis=0)

    def softplus_terms(j, r0):
        z = lax.dot_general(q[r0:], stacked(k_ref, j), (((1,), (1,)), ((), ())), preferred_element_type=F32)
        neg_abs = pltpu.bitcast(pltpu.bitcast(z, jnp.uint32) | jnp.uint32(0x80000000), F32)
        sp = jnp.maximum(z, 0.0) + jnp.log2(1.0 + jnp.exp2(neg_abs))
        return z - sp, sp

    def attend(j, r0, log_beta, sp16, diagonal):
        wp = jnp.dot(sp16, later_mat, preferred_element_type=F32)
        tp = jnp.dot(sp16, total_mat, preferred_element_type=F32)
        c = carry[r0:, :]
        att = jnp.exp2(log_beta - (wp + c))
        if diagonal:
            att = mask_diagonal(att)
        carry[r0:, :] = c + tp
        acc[r0:, :] += jnp.dot(att.astype(BF16), stacked(v_ref, j), preferred_element_type=F32)

    def step(j, r0, diagonal):
        log_beta, sp = softplus_terms(j, r0)
        if diagonal:
            sp = mask_diagonal(sp)
        attend(j, r0, log_beta, sp.astype(BF16), diagonal)

    def stage(j):
        log_beta, sp = softplus_terms(j, 0)
        lb_buf[...] = log_beta
        sp_buf[...] = sp.astype(BF16)

    carry[...] = jnp.zeros_like(carry)
    acc[...] = jnp.zeros_like(acc)
    for kb in range(ATTN_Q_BLOCKS - 1, -1, -1):
        step(first_blk + kb, kb * blk, True)

    stage(jnp.maximum(first_blk - 1, 0))

    def body(t, _):
        j = first_blk - 1 - t * ATTN_UNROLL
        attend(j, 0, lb_buf[...], sp_buf[...], False)
        for u in range(1, ATTN_UNROLL):
            step(j - u, 0, False)
        stage(jnp.maximum(j - ATTN_UNROLL, 0))
        return 0

    lax.fori_loop(0, first_blk // ATTN_UNROLL, body, 0)
    o_ref[...] = acc[...].astype(BF16)


def _attention(qkv, bsz, seq_len):
    n = qkv.shape[0]
    rows = ATTN_Q_BLOCKS * SB_BLOCK
    steps = seq_len // rows
    pairs = SB_WIDTH // LANES
    return pl.pallas_call(
        _attn_kernel,
        grid=(bsz, pairs, steps),
        in_specs=[
            pl.BlockSpec((rows, LANES), lambda b, p, i: (b * steps + i, p)),
            pl.BlockSpec((seq_len, LANES), lambda b, p, i: (b, pairs + p)),
            pl.BlockSpec((seq_len, LANES), lambda b, p, i: (b, 2 * pairs + p)),
        ],
        out_specs=pl.BlockSpec((rows, LANES), lambda b, p, i: (b * steps + i, p)),
        out_shape=jax.ShapeDtypeStruct((n, SB_WIDTH), BF16),
        scratch_shapes=[pltpu.VMEM((rows, LANES), F32), pltpu.VMEM((rows, 2 * SB_BLOCK), F32),
                        pltpu.VMEM((rows, 2 * SB_BLOCK), F32), pltpu.VMEM((rows, 2 * SB_BLOCK), BF16)],
        compiler_params=_compiler_params(3),
        name="stickbreak_attention",
    )(qkv, qkv, qkv)


def _ssd_chunks(first, z_ref, xbc_ref, dt_ref, cw_ref, cb_ref, dtb_ref, alog_ref, dskip_ref, nw_ref,
                y_ref, xbuf, state, after_chunk):
    lc = SSM_CHUNK
    rows = z_ref.shape[0]

    @pl.when(first)
    def _():
        xbuf[0:HALO, :] = jnp.zeros((HALO, SSM_CONV_DIM), F32)
        state[...] = jnp.zeros_like(state)

    @pl.when(jnp.logical_not(first))
    def _():
        xbuf[0:HALO, :] = xbuf[rows:rows + HALO, :]

    xbuf[HALO:, :] = xbc_ref[...].astype(F32)
    a_neg = -jnp.exp(alog_ref[...])
    ri = lax.broadcasted_iota(jnp.int32, (lc, lc), 0)
    ci = lax.broadcasted_iota(jnp.int32, (lc, lc), 1)
    lower = ri >= ci
    tril = jnp.where(lower, 1.0, 0.0)
    lane = lax.broadcasted_iota(jnp.int32, (lc, LANES), 1)
    low_half = lane < SSM_HEAD_DIM

    def expand(col_src):
        parts = []
        for p in range(SSM_HEADS // 2):
            lo = jnp.broadcast_to(col_src[:, 2 * p:2 * p + 1], (lc, LANES))
            hi = jnp.broadcast_to(col_src[:, 2 * p + 1:2 * p + 2], (lc, LANES))
            parts.append(jnp.where(low_half, lo, hi))
        return jnp.concatenate(parts, axis=1)

    for r0 in range(0, rows, lc):
        conv = cb_ref[...] + cw_ref[SSM_CONV - 1:SSM_CONV, :] * xbuf[HALO + r0:HALO + r0 + lc, :]
        for k in range(SSM_CONV - 1):
            back = SSM_CONV - 1 - k
            conv = conv + cw_ref[k:k + 1, :] * xbuf[HALO - back + r0:HALO - back + r0 + lc, :]
        xc = conv * _sigmoid(conv)
        xs = xc[:, :SSM_INNER]
        bm16 = xc[:, SSM_INNER:SSM_INNER + LANES].astype(BF16)
        cm16 = xc[:, SSM_INNER + LANES:].astype(BF16)

        dt = _softplus(dt_ref[r0:r0 + lc, :] + dtb_ref[...])
        a = dt * a_neg
        a_cs = jnp.dot(tril, a, preferred_element_type=F32,
                       precision=lax.Precision.HIGHEST)
        a_cs_t = a_cs.T

        dt_x = expand(dt)
        acs_x = expand(a_cs)
        xdt = xs * dt_x
        last_x = acs_x[lc - 1:lc, :]
        decay_to_end = jnp.exp(last_x - acs_x)
        decay_from_start = jnp.exp(acs_x)
        chunk_decay = jnp.exp(last_x)

        cb = []
        for g in range(SSM_GROUPS):
            cg = jnp.where(low_half if g == 0 else ~low_half, cm16, jnp.zeros_like(cm16))
            cb.append(lax.dot_general(cg, bm16, (((1,), (1,)), ((), ())), preferred_element_type=F32))
        xdt16 = xdt.astype(BF16)
        y_pairs = []
        for p in range(SSM_HEADS // 2):
            xp = xdt16[:, p * LANES:(p + 1) * LANES]
            yp = None
            for half in range(2):
                hd = 2 * p + half
                g = hd // (SSM_HEADS // SSM_GROUPS)
                diff = a_cs[:, hd:hd + 1] - a_cs_t[hd:hd + 1, :]
                dec = jnp.exp(jnp.where(lower, diff, -jnp.inf))
                scores = (cb[g] * dec).astype(BF16)
                xh = jnp.where(low_half if half == 0 else ~low_half, xp, jnp.zeros_like(xp))
                t = jnp.dot(scores, xh, preferred_element_type=F32)
                yp = t if yp is None else yp + t
            y_pairs.append(yp)
        y = jnp.concatenate(y_pairs, axis=1)

        prev = state[...]
        y = y + jnp.dot(cm16, prev.astype(BF16), preferred_element_type=F32) * decay_from_start
        xw = (xdt * decay_to_end).astype(BF16)
        cs = lax.dot_general(bm16, xw, (((0,), (0,)), ((), ())), preferred_element_type=F32)
        srow = lax.broadcasted_iota(jnp.int32, cs.shape, 0) // SSM_STATE
        scol = lax.broadcasted_iota(jnp.int32, cs.shape, 1) // (SSM_INNER // SSM_GROUPS)
        state[...] = prev * chunk_decay + jnp.where(srow == scol, cs, 0.0)

        y = y + xs * dskip_ref[...]
        zf = z_ref[r0:r0 + lc, :].astype(F32)
        y = y * (zf * _sigmoid(zf))
        gw = SSM_INNER // SSM_GROUPS
        outs = []
        for g in range(SSM_GROUPS):
            yg = y[:, g * gw:(g + 1) * gw]
            ms = jnp.mean(yg * yg, axis=-1, keepdims=True)
            outs.append(yg * lax.rsqrt(ms + NORM_EPS))
        y_ref[r0:r0 + lc, :] = (jnp.concatenate(outs, axis=1) * nw_ref[...]).astype(BF16)
        after_chunk(r0 + lc)


def _ssd_merge_kernel(tiles_per_seq, x_ref, mod_ref, gpre_ref, gpost_ref, fa_ref, fb_ref,
                      z_ref, xbc_ref, dt_ref, cw_ref, cb_ref, dtb_ref, alog_ref, dskip_ref, nw_ref,
                      wg_ref, wa_ref, wb_ref, wc_ref, wo_ref, o_ref, fc_buf, xbuf, state):
    d = D_MODEL
    shift = mod_ref[0, 0:1, :]
    scale = mod_ref[0, 1:2, :]
    gate = mod_ref[0, 2:3, :]
    first = pl.program_id(0) % tiles_per_seq == 0
    tiles = _sub_tiles(x_ref.shape[0])

    def merge_tile(rs):
        x = x_ref[rs, :]
        h = _modulated_prenorm(x, gpre_ref[...], scale, shift).astype(BF16)
        merged = None
        for br, (f_ref, w_ref) in enumerate(((fa_ref, wa_ref), (fb_ref, wb_ref), (fc_buf, wc_ref))):
            gl = jnp.dot(h, wg_ref[:, br * d:(br + 1) * d], preferred_element_type=F32)
            yb = jnp.dot(f_ref[rs, :], w_ref[...], preferred_element_type=F32)
            t = _sigmoid(gl) * yb
            merged = t if merged is None else merged + t
        mix = jnp.dot(merged.astype(BF16), wo_ref[...], preferred_element_type=F32)
        ms = jnp.mean(mix * mix, axis=-1, keepdims=True)
        o_ref[rs, :] = x + gate * (mix * lax.rsqrt(ms + NORM_EPS) * gpost_ref[...])

    def after_chunk(done_rows):
        for rs in tiles:
            if rs.stop == done_rows:
                merge_tile(rs)

    _ssd_chunks(first, z_ref, xbc_ref, dt_ref, cw_ref, cb_ref, dtb_ref, alog_ref, dskip_ref, nw_ref,
                fc_buf, xbuf, state, after_chunk)


def _ssd_merge(xf, mod_l, g_pre, g_post, fa, fb, z, xbc, dt, conv_w, conv_b, dt_bias, a_log, d_skip, norm_w,
               wg, wa, wb, wc, wo, seq_len):
    n, d = xf.shape
    tm = min(TOKEN_TILE, seq_len)
    assert tm % SSM_CHUNK == 0 and min(SUB_TILE, tm) % SSM_CHUNK == 0
    tiles_per_seq = seq_len // tm
    row = lambda i: (i, 0)
    return pl.pallas_call(
        functools.partial(_ssd_merge_kernel, tiles_per_seq),
        grid=(n // tm,),
        in_specs=[
            pl.BlockSpec((tm, d), row),
            pl.BlockSpec((1, N_MOD, d), lambda i: (i // tiles_per_seq, 0, 0)),
            _resident((1, d)),
            _resident((1, d)),
            pl.BlockSpec((tm, SC_WIDTH), row),
            pl.BlockSpec((tm, SB_WIDTH), row),
            pl.BlockSpec((tm, SSM_INNER), row),
            pl.BlockSpec((tm, SSM_CONV_DIM), row),
            pl.BlockSpec((tm, LANES), row),
            _resident((SSM_CONV, SSM_CONV_DIM)),
            _resident((1, SSM_CONV_DIM)),
            _resident((1, LANES)),
            _resident((1, LANES)),
            _resident((1, SSM_INNER)),
            _resident((1, SSM_INNER)),
            _resident((d, 3 * d)),
            _resident((SC_WIDTH, d)),
            _resident((SB_WIDTH, d)),
            _resident((SSM_INNER, d)),
            _resident((d, d)),
        ],
        out_specs=pl.BlockSpec((tm, d), row),
        out_shape=jax.ShapeDtypeStruct((n, d), F32),
        scratch_shapes=[pltpu.VMEM((tm, SSM_INNER), BF16),
                        pltpu.VMEM((tm + HALO, SSM_CONV_DIM), F32),
                        pltpu.VMEM((SSM_GROUPS * SSM_STATE, SSM_INNER), F32)],
        compiler_params=_compiler_params(1),
        name="ssd_merge",
    )(xf, mod_l, g_pre, g_post, fa, fb, z, xbc, dt, conv_w, conv_b, dt_bias, a_log, d_skip, norm_w,
      wg, wa, wb, wc, wo)


def _ffn_kernel(x_ref, mod_ref, gpre_ref, gpost_ref, win_ref, wout_ref, o_ref):
    shift = mod_ref[0, 3:4, :]
    scale = mod_ref[0, 4:5, :]
    gate = mod_ref[0, 5:6, :]
    for rs in _sub_tiles(x_ref.shape[0]):
        x = x_ref[rs, :]
        h = _modulated_prenorm(x, gpre_ref[...], scale, shift).astype(BF16)
        gt = jnp.dot(h, win_ref[:, :FFN_HIDDEN], preferred_element_type=F32)
        up = jnp.dot(h, win_ref[:, FFN_HIDDEN:], preferred_element_type=F32)
        act = (gt * _sigmoid(gt) * up).astype(BF16)
        f = jnp.dot(act, wout_ref[...], preferred_element_type=F32)
        ms = jnp.mean(f * f, axis=-1, keepdims=True)
        o_ref[rs, :] = x + gate * (f * lax.rsqrt(ms + NORM_EPS) * gpost_ref[...])


def _ffn(xf, mod_l, g_pre, g_post, w_in, w_out, seq_len):
    n, d = xf.shape
    tm = min(TOKEN_TILE, seq_len)
    tiles_per_seq = seq_len // tm
    row = lambda i: (i, 0)
    return pl.pallas_call(
        _ffn_kernel,
        grid=(n // tm,),
        in_specs=[
            pl.BlockSpec((tm, d), row),
            pl.BlockSpec((1, N_MOD, d), lambda i: (i // tiles_per_seq, 0, 0)),
            _resident((1, d)),
            _resident((1, d)),
            _resident((d, 2 * FFN_HIDDEN)),
            _resident((FFN_HIDDEN, d)),
        ],
        out_specs=pl.BlockSpec((tm, d), row),
        out_shape=jax.ShapeDtypeStruct((n, d), F32),
        compiler_params=_compiler_params(1),
        name="swiglu_ffn",
    )(xf, mod_l, g_pre, g_post, w_in, w_out)


def _pad_lanes(v):
    return jnp.pad(v.astype(F32), (0, LANES - v.shape[0])).reshape(1, LANES)


def kernel(x, c, mod_w, mod_b, g_pre_mix, g_post_mix, g_pre_ffn, g_post_ffn, w_in, sc_conv_w,
           ssm_conv_w, ssm_conv_b, ssm_dt_bias, ssm_a_log, ssm_d, ssm_norm_w, w_sc_out, w_sb_out,
           w_ssm_out, w_o, w_ffn_in, w_ffn_out):
    bsz, seq_len, d = x.shape
    depth = mod_w.shape[0]
    assert d == D_MODEL and seq_len % (ATTN_Q_BLOCKS * SB_BLOCK) == 0 and seq_len % SSM_CHUNK == 0
    mod = _modulation(c, mod_w, mod_b).reshape(depth, bsz, N_MOD, d)
    xf = x.reshape(bsz * seq_len, d)
    cols = jnp.arange(_W1_COLS)
    q_cols = (cols >= _OFF_QKV) & (cols < _OFF_QKV + SB_WIDTH)
    col_scale = jnp.where(q_cols, SB_HEAD_DIM ** -0.5 * LOG2_E, 1.0).astype(F32).reshape(1, _W1_COLS)
    for l in range(depth):
        wg = w_in[l][:, _OFF_DT + SSM_HEADS:].astype(BF16)
        row = lambda v: v.reshape(1, -1).astype(F32)

        fa, qkv, z, xbc, dt = _inproj(xf, mod[l], row(g_pre_mix[l]), w_in, l, col_scale,
                                      sc_conv_w[l].astype(F32), seq_len)
        fb = _attention(qkv, bsz, seq_len)
        xf = _ssd_merge(xf, mod[l], row(g_pre_mix[l]), row(g_post_mix[l]), fa, fb, z, xbc, dt,
                        ssm_conv_w[l].astype(F32), row(ssm_conv_b[l]), _pad_lanes(ssm_dt_bias[l]),
                        _pad_lanes(ssm_a_log[l]), row(jnp.repeat(ssm_d[l], SSM_HEAD_DIM)), row(ssm_norm_w[l]),
                        wg, w_sc_out[l].astype(BF16), w_sb_out[l].astype(BF16), w_ssm_out[l].astype(BF16),
                        w_o[l].astype(BF16), seq_len)
        xf = _ffn(xf, mod[l], row(g_pre_ffn[l]), row(g_post_ffn[l]), w_ffn_in[l].astype(BF16),
                  w_ffn_out[l].astype(BF16), seq_len)
    return xf.reshape(bsz, seq_len, d)
```
